```python
import jax, jax.numpy as jnp
from jax import lax
import numpy as np

D_MODEL = 1024
BATCH = 16
SEQ = 4096
DEPTH = 1
DEC_BATCH = 128
DEC_SEQ = 4
PAST_LEN = 8192
PAGE_SIZE = 128

HEAD_DIM = 64
N_ATTN_HEADS = 8
ATTN_WIDTH = N_ATTN_HEADS * HEAD_DIM
CONV_CH = D_MODEL - ATTN_WIDTH
IN_COLS = 3 * ATTN_WIDTH + 2 * CONV_CH
CONV_KERNEL = 31
DIL_CONFIGS = ((128, 1), (512, 4), (2048, 16))
MAX_WINDOW = 2048
KEYS_PER_BRANCH = 128
N_MEM = 256
N_XHEADS = 4
XHEAD_DIM = D_MODEL // N_XHEADS
D_FF = 2816
FFN_CONV = 3
EPS = 1e-6

kernel_name = "hymba_conformer_dilated_swa_decoder_step"


def rms_norm(x, g):
    xf = x.astype(jnp.float32)
    y = xf * lax.rsqrt(jnp.mean(xf * xf, axis=-1, keepdims=True) + EPS)
    return (y * g.astype(jnp.float32)).astype(x.dtype)


def layer_norm(x, g, b):
    xf = x.astype(jnp.float32)
    mu = jnp.mean(xf, axis=-1, keepdims=True)
    var = jnp.mean(jnp.square(xf - mu), axis=-1, keepdims=True)
    y = (xf - mu) * lax.rsqrt(var + EPS) * g.astype(jnp.float32) + b.astype(jnp.float32)
    return y.astype(x.dtype)


def causal_dwconv(x_ctx, w):
    return lax.conv_general_dilated(x_ctx, w[:, None, :], window_strides=(1,), padding="VALID",
                                    dimension_numbers=("NWC", "WIO", "NWC"),
                                    feature_group_count=x_ctx.shape[-1])


def dilated_branch_prompt(q, k, v, dil):
    B, S, H, Dh = q.shape
    blk = KEYS_PER_BRANCH
    span = dil * blk
    sp = -(-S // span) * span
    nb = sp // span
    pad = ((0, 0), (0, sp - S), (0, 0), (0, 0))

    def to_blocks(t):
        return jnp.pad(t, pad).reshape(B, nb, blk, dil, H, Dh)

    def with_prev(t):
        prev = jnp.concatenate([jnp.zeros_like(t[:, :1]), t[:, :-1]], axis=1)
        return jnp.concatenate([prev, t], axis=2)

    qb = to_blocks(q)
    kk = with_prev(to_blocks(k))
    vv = with_prev(to_blocks(v))
    s = jnp.einsum("bnqrhd,bnkrhd->bnrhqk", qb, kk).astype(jnp.float32) * (HEAD_DIM ** -0.5)
    lq = jnp.arange(blk)[:, None]
    lk = jnp.arange(2 * blk)[None, :]
    band = (lk >= lq) & (lk <= lq + blk)
    not_first = (jnp.arange(nb)[:, None, None] > 0) | (lk >= blk)[None]
    mask = band[None] & not_first
    s = jnp.where(mask[None, :, None, None], s, -jnp.inf)
    m = jnp.max(s, axis=-1, keepdims=True)
    p = jnp.exp(s - m)
    den = jnp.sum(p, axis=-1, keepdims=True)
    o = jnp.einsum("bnrhqk,bnkrhd->bnqrhd", (p / den).astype(v.dtype), vv)
    lse = (m + jnp.log(den))[..., 0]
    o = o.reshape(B, sp, H, Dh)[:, :S]
    lse = jnp.transpose(lse, (0, 1, 4, 2, 3)).reshape(B, sp, H)[:, :S]
    return o, lse


def dilated_branch_sample(q, kc, vc, dil):
    T = q.shape[1]
    wb = kc.shape[1] - T
    j = jnp.arange(KEYS_PER_BRANCH + 1)
    idx = wb + jnp.arange(T)[:, None] - dil * j[None, :]
    valid = idx >= 0
    idxc = jnp.maximum(idx, 0)
    kg = kc[:, idxc]
    vg = vc[:, idxc]
    s = jnp.einsum("bthd,btjhd->bthj", q, kg).astype(jnp.float32) * (HEAD_DIM ** -0.5)
    s = jnp.where(valid[None, :, None, :], s, -jnp.inf)
    m = jnp.max(s, axis=-1, keepdims=True)
    p = jnp.exp(s - m)
    den = jnp.sum(p, axis=-1, keepdims=True)
    o = jnp.einsum("bthj,btjhd->bthd", (p / den).astype(vc.dtype), vg)
    lse = (m + jnp.log(den))[..., 0]
    return o, lse


def combine_branches(outs, lses):
    w = jax.nn.softmax(jnp.stack(lses, axis=0), axis=0)
    o = jnp.stack(outs, axis=0)
    return jnp.einsum("gbth,gbthd->bthd", w.astype(o.dtype), o)


def memory_kv(mem, p):
    Bm, M, _ = mem.shape
    mn = rms_norm(mem, p["g_mem"])
    mk = (mn @ p["w_mk"]).reshape(Bm, M, N_XHEADS, XHEAD_DIM)
    mv = (mn @ p["w_mv"]).reshape(Bm, M, N_XHEADS, XHEAD_DIM)
    return mk, mv


def trunk_layer(x, is_prompt, win_k, win_v, conv_ctx, ffn_ctx, mem_k, mem_v, p):
    B, T, _ = x.shape
    h = rms_norm(x, p["g_pre_mix"])
    z = h @ p["w_in"]
    q, k, v, a, gate = jnp.split(z, [ATTN_WIDTH, 2 * ATTN_WIDTH, 3 * ATTN_WIDTH,
                                     3 * ATTN_WIDTH + CONV_CH], axis=-1)
    q = q.reshape(B, T, N_ATTN_HEADS, HEAD_DIM)
    k = k.reshape(B, T, N_ATTN_HEADS, HEAD_DIM)
    v = v.reshape(B, T, N_ATTN_HEADS, HEAD_DIM)
    outs, lses = [], []
    if is_prompt:
        for _, dil in DIL_CONFIGS:
            o_i, l_i = dilated_branch_prompt(q, k, v, dil)
            outs.append(o_i)
            lses.append(l_i)
        wb = min(MAX_WINDOW, T)
        new_k, new_v = k[:, T - wb:], v[:, T - wb:]
    else:
        kc = jnp.concatenate([win_k, k], axis=1)
        vc = jnp.concatenate([win_v, v], axis=1)
        for _, dil in DIL_CONFIGS:
            o_i, l_i = dilated_branch_sample(q, kc, vc, dil)
            outs.append(o_i)
            lses.append(l_i)
        new_k, new_v = k, v
    attn = combine_branches(outs, lses).reshape(B, T, ATTN_WIDTH)
    u = a * jax.nn.sigmoid(gate)
    u_ctx = jnp.concatenate([conv_ctx, u], axis=1)
    c = causal_dwconv(u_ctx, p["conv_w"]) + p["conv_b"]
    c = jax.nn.silu(layer_norm(c, p["ln_conv_g"], p["ln_conv_b"]))
    new_conv = u_ctx[:, -(CONV_KERNEL - 1):]
    mix = jnp.concatenate([rms_norm(attn, p["g_out_attn"]), rms_norm(c, p["g_out_conv"])], axis=-1)
    x = x + rms_norm(mix @ p["w_out"], p["g_post_mix"])
    h = rms_norm(x, p["g_pre_x"])
    qx = (h @ p["w_xq"]).reshape(B, T, N_XHEADS, XHEAD_DIM)
    s = jnp.einsum("bthd,bmhd->bhtm", qx, mem_k).astype(jnp.float32) * (XHEAD_DIM ** -0.5)
    pm = jax.nn.softmax(s, axis=-1)
    ox = jnp.einsum("bhtm,bmhd->bthd", pm.astype(mem_v.dtype), mem_v).reshape(B, T, D_MODEL)
    x = x + rms_norm(ox @ p["w_xo"], p["g_post_x"])
    h = rms_norm(x, p["g_pre_ffn"])
    up = h @ p["w_up"]
    act_in, lin = jnp.split(up, [D_FF], axis=-1)
    f_ctx = jnp.concatenate([ffn_ctx, act_in], axis=1)
    f = jax.nn.gelu(causal_dwconv(f_ctx, p["ffn_conv_w"])) * lin
    new_ffn = f_ctx[:, -(FFN_CONV - 1):]
    x = x + rms_norm(f @ p["w_down"], p["g_post_ffn"])
    return x, new_k, new_v, new_conv, new_ffn


def setup_inputs(seed: int = 0) -> dict:
    key = jax.random.key(seed)
    ks = jax.random.split(key, 40)
    f32 = jnp.float32

    def nrm(k, shape, scale):
        return jax.random.normal(k, shape, f32) * scale

    def gain(k, n):
        return 1.0 + 0.01 * jax.random.normal(k, (DEPTH, n), f32)

    wb = min(MAX_WINDOW, PAST_LEN)
    return {
        "x_prompt": nrm(ks[0], (BATCH, SEQ, D_MODEL), 1.0),
        "x_sample": nrm(ks[1], (DEC_BATCH, DEC_SEQ, D_MODEL), 1.0),
        "mem_prompt": nrm(ks[2], (BATCH, N_MEM, D_MODEL), 1.0),
        "cache_win_k": nrm(ks[3], (DEPTH, DEC_BATCH, wb, N_ATTN_HEADS, HEAD_DIM), 1.0),
        "cache_win_v": nrm(ks[4], (DEPTH, DEC_BATCH, wb, N_ATTN_HEADS, HEAD_DIM), 1.0),
        "state_conv": nrm(ks[5], (DEPTH, DEC_BATCH, CONV_KERNEL - 1, CONV_CH), 0.5),
        "state_ffn_conv": nrm(ks[6], (DEPTH, DEC_BATCH, FFN_CONV - 1, D_FF), 1.0),
        "cache_mem_k": nrm(ks[7], (DEPTH, DEC_BATCH, N_MEM, N_XHEADS, XHEAD_DIM), 1.0),
        "cache_mem_v": nrm(ks[8], (DEPTH, DEC_BATCH, N_MEM, N_XHEADS, XHEAD_DIM), 1.0),
        "g_pre_mix": gain(ks[9], D_MODEL),
        "w_in": nrm(ks[10], (DEPTH, D_MODEL, IN_COLS), D_MODEL ** -0.5),
        "conv_w": nrm(ks[11], (DEPTH, CONV_KERNEL, CONV_CH), CONV_KERNEL ** -0.5),
        "conv_b": nrm(ks[12], (DEPTH, CONV_CH), 0.01),
        "ln_conv_g": gain(ks[13], CONV_CH),
        "ln_conv_b": nrm(ks[14], (DEPTH, CONV_CH), 0.01),
        "g_out_attn": gain(ks[15], ATTN_WIDTH),
        "g_out_conv": gain(ks[16], CONV_CH),
        "w_out": nrm(ks[17], (DEPTH, D_MODEL, D_MODEL), D_MODEL ** -0.5),
        "g_post_mix": gain(ks[18], D_MODEL),
        "g_pre_x": gain(ks[19], D_MODEL),
        "g_mem": gain(ks[20], D_MODEL),
        "w_xq": nrm(ks[21], (DEPTH, D_MODEL, D_MODEL), D_MODEL ** -0.5),
        "w_mk": nrm(ks[22], (DEPTH, D_MODEL, D_MODEL), D_MODEL ** -0.5),
        "w_mv": nrm(ks[23], (DEPTH, D_MODEL, D_MODEL), D_MODEL ** -0.5),
        "w_xo": nrm(ks[24], (DEPTH, D_MODEL, D_MODEL), D_MODEL ** -0.5),
        "g_post_x": gain(ks[25], D_MODEL),
        "g_pre_ffn": gain(ks[26], D_MODEL),
        "w_up": nrm(ks[27], (DEPTH, D_MODEL, 2 * D_FF), D_MODEL ** -0.5),
        "ffn_conv_w": nrm(ks[28], (DEPTH, FFN_CONV, D_FF), FFN_CONV ** -0.5),
        "w_down": nrm(ks[29], (DEPTH, D_FF, D_MODEL), D_FF ** -0.5),
        "g_post_ffn": gain(ks[30], D_MODEL),
    }


def reference(x_prompt, x_sample, mem_prompt, cache_win_k, cache_win_v, state_conv, state_ffn_conv,
              cache_mem_k, cache_mem_v, g_pre_mix, w_in, conv_w, conv_b, ln_conv_g, ln_conv_b,
              g_out_attn, g_out_conv, w_out, g_post_mix, g_pre_x, g_mem, w_xq, w_mk, w_mv, w_xo,
              g_post_x, g_pre_ffn, w_up, ffn_conv_w, w_down, g_post_ffn):
    y_p, y_s = x_prompt, x_sample
    wk_p, wv_p, cv_p, ff_p, mk_p, mv_p = [], [], [], [], [], []
    wk_s, wv_s, cv_s, ff_s = [], [], [], []
    for l in range(DEPTH):
        p = {
            "g_pre_mix": g_pre_mix[l], "w_in": w_in[l], "conv_w": conv_w[l], "conv_b": conv_b[l],
            "ln_conv_g": ln_conv_g[l], "ln_conv_b": ln_conv_b[l], "g_out_attn": g_out_attn[l],
            "g_out_conv": g_out_conv[l], "w_out": w_out[l], "g_post_mix": g_post_mix[l],
            "g_pre_x": g_pre_x[l], "g_mem": g_mem[l], "w_xq": w_xq[l], "w_mk": w_mk[l],
            "w_mv": w_mv[l], "w_xo": w_xo[l], "g_post_x": g_post_x[l], "g_pre_ffn": g_pre_ffn[l],
            "w_up": w_up[l], "ffn_conv_w": ffn_conv_w[l], "w_down": w_down[l], "g_post_ffn": g_post_ffn[l],
        }
        mk, mv = memory_kv(mem_prompt, p)
        zc = jnp.zeros((y_p.shape[0], CONV_KERNEL - 1, CONV_CH), y_p.dtype)
        zf = jnp.zeros((y_p.shape[0], FFN_CONV - 1, D_FF), y_p.dtype)
        y_p, nk, nv, nc, nf = trunk_layer(y_p, True, None, None, zc, zf, mk, mv, p)
        wk_p.append(nk); wv_p.append(nv); cv_p.append(nc); ff_p.append(nf); mk_p.append(mk); mv_p.append(mv)
        y_s, nk, nv, nc, nf = trunk_layer(y_s, False, cache_win_k[l], cache_win_v[l], state_conv[l],
                                          state_ffn_conv[l], cache_mem_k[l], cache_mem_v[l], p)
        wk_s.append(nk); wv_s.append(nv); cv_s.append(nc); ff_s.append(nf)
    return (y_p, y_s,
            jnp.stack(wk_p), jnp.stack(wv_p), jnp.stack(cv_p), jnp.stack(ff_p), jnp.stack(mk_p), jnp.stack(mv_p),
            jnp.stack(wk_s), jnp.stack(wv_s), jnp.stack(cv_s), jnp.stack(ff_s))
```

```python
import functools
import math

import jax
import jax.numpy as jnp
from jax import lax
from jax.experimental import pallas as pl
from jax.experimental.pallas import tpu as pltpu

F32 = jnp.float32
BF16 = jnp.bfloat16

EPS = 1e-6
D_MODEL = 1024
N_HEADS = 8
HEAD_DIM = 64
ATTN_W = N_HEADS * HEAD_DIM
CONV_CH = D_MODEL - ATTN_W
CONV_K = 31
BLK = 128
DILATIONS = (16, 4, 1)
MAX_WINDOW = 2048
N_MEM = 256
N_XHEADS = 4
XHEAD_DIM = D_MODEL // N_XHEADS
D_FF = 2816
FFN_K = 3
T_NEW = 4
NEG = -1e30
ML_LANES = 128
FF_CHUNK = 256
VMEM_LIMIT = 56 * 1024 * 1024


def _rms(x, g):
    return x * lax.rsqrt(jnp.mean(x * x, axis=-1, keepdims=True) + EPS) * g


def _dot(a, b):
    return jnp.dot(a, b, preferred_element_type=F32)


def _dot_nt(a, b):
    return lax.dot_general(a, b, (((1,), (1,)), ((), ())), preferred_element_type=F32)


def _sigmoid(x):
    return 1.0 / (1.0 + jnp.exp(-x))


def _gelu_tanh(x):
    c = math.sqrt(2.0 / math.pi)
    return x * (0.5 * (1.0 + jnp.tanh(c * (x + 0.044715 * (x * x * x)))))


def _split_bf16(x):
    hi = x.astype(BF16)
    lo = (x - hi.astype(F32)).astype(BF16)
    return hi, lo


def _in_proj(h, w_ref):
    q = _dot(h, w_ref[:, 0:ATTN_W]) * (HEAD_DIM ** -0.5)
    k = _dot(h, w_ref[:, ATTN_W:2 * ATTN_W])
    v = _dot(h, w_ref[:, 2 * ATTN_W:3 * ATTN_W])
    a = _dot(h, w_ref[:, 3 * ATTN_W:3 * ATTN_W + CONV_CH])
    gate = _dot(h, w_ref[:, 3 * ATTN_W + CONV_CH:3 * ATTN_W + 2 * CONV_CH])
    return q, k, v, a * _sigmoid(gate)


def _conv_post(c, lng, lnb):
    mu = jnp.mean(c, axis=-1, keepdims=True)
    d = c - mu
    var = jnp.mean(d * d, axis=-1, keepdims=True)
    y = d * lax.rsqrt(var + EPS) * lng + lnb
    return y * _sigmoid(y)


def _mix_out(attn, c, goa, goc, wout_ref):
    ma = _rms(attn, goa).astype(BF16)
    mc = _rms(c, goc).astype(BF16)
    return _dot(ma, wout_ref[0:ATTN_W, :]) + _dot(mc, wout_ref[ATTN_W:D_MODEL, :])


def _ffn_act(f, lin):
    return (_gelu_tanh(f) * lin).astype(BF16)


def _const_spec(shape):
    nd = len(shape)
    return pl.BlockSpec(shape, lambda *_: (0,) * nd, pipeline_mode=pl.Buffered(1))


def _params(sem):
    return pltpu.CompilerParams(dimension_semantics=sem, vmem_limit_bytes=VMEM_LIMIT)


def _row(v):
    return v.reshape(1, -1).astype(F32)


def _p_inproj_kernel(x_ref, g_ref, w_ref, q_ref, k_ref, v_ref, kw_ref, vw_ref, u_ref, *, first_win_tile):
    j = pl.program_id(1)
    h = _rms(x_ref[...], g_ref[...]).astype(BF16)
    q, k, v, u = _in_proj(h, w_ref)
    q_ref[...] = q.astype(BF16)
    k_ref[...] = k.astype(BF16)
    v_ref[...] = v.astype(BF16)
    u_ref[...] = u

    @pl.when(j >= first_win_tile)
    def _():
        kw_ref[...] = k
        vw_ref[...] = v


def _p_inproj(x, g, w_in, tm):
    B, S, _ = x.shape
    W = min(MAX_WINDOW, S)
    nt = S // tm
    first = (S - W) // tm
    tok = lambda c: pl.BlockSpec((None, tm, c), lambda b, j: (b, j, 0))
    win = pl.BlockSpec((None, tm, ATTN_W), lambda b, j: (b, jnp.maximum(j - first, 0), 0))
    return pl.pallas_call(
        functools.partial(_p_inproj_kernel, first_win_tile=first),
        grid=(B, nt),
        in_specs=[tok(D_MODEL), _const_spec((1, D_MODEL)), _const_spec(w_in.shape)],
        out_specs=[tok(ATTN_W), tok(ATTN_W), tok(ATTN_W), win, win, tok(CONV_CH)],
        out_shape=[jax.ShapeDtypeStruct((B, S, ATTN_W), BF16)] * 3
        + [jax.ShapeDtypeStruct((B, W, ATTN_W), F32)] * 2
        + [jax.ShapeDtypeStruct((B, S, CONV_CH), F32)],
        compiler_params=_params(("arbitrary", "arbitrary")),
        name="p_inproj",
    )(x, g, w_in)


def _band_attn_kernel(*refs, tq, has_in, final):
    q_ref, ko_ref, kp_ref, vo_ref, vp_ref = refs[:5]
    pos = 5
    if has_in:
        mi_ref, li_ref, ai_ref = refs[pos:pos + 3]
        pos += 3
    if final:
        o_ref = refs[pos]
    else:
        mo_ref, lo_ref, ao_ref = refs[pos:pos + 3]

    i = pl.program_id(2)
    row = lax.broadcasted_iota(jnp.int32, (BLK, 2 * BLK), 0)
    col = lax.broadcasted_iota(jnp.int32, (BLK, 2 * BLK), 1)
    band = (col >= row) & (col <= row + BLK)
    band_first = band & ((col >= BLK) | (i > 0))
    lane = lax.broadcasted_iota(jnp.int32, (BLK, ML_LANES), 1)

    for n in range(tq // BLK):
        rows = slice(n * BLK, (n + 1) * BLK)
        if has_in:
            m_in = mi_ref[rows, :]
            l_in = li_ref[rows, :]
        m_tile = jnp.zeros((BLK, ML_LANES), F32)
        l_tile = jnp.zeros((BLK, ML_LANES), F32)
        for h in range(N_HEADS):
            hs = slice(h * HEAD_DIM, (h + 1) * HEAD_DIM)
            qh = q_ref[rows, hs]
            if n == 0:
                kk = jnp.concatenate([kp_ref[:, hs], ko_ref[0:BLK, hs]], axis=0)
                vv = jnp.concatenate([vp_ref[:, hs], vo_ref[0:BLK, hs]], axis=0)
                mask = band_first
            else:
                kk = ko_ref[(n - 1) * BLK:(n + 1) * BLK, hs]
                vv = vo_ref[(n - 1) * BLK:(n + 1) * BLK, hs]
                mask = band
            s = jnp.where(mask, _dot_nt(qh, kk), NEG)
            m = jnp.max(s, axis=-1, keepdims=True)
            if has_in:
                m0 = jnp.max(jnp.where(lane == h, m_in, NEG), axis=-1, keepdims=True)
                l0 = jnp.sum(jnp.where(lane == h, l_in, 0.0), axis=-1, keepdims=True)
                m_new = jnp.maximum(m0, m)
            else:
                m_new = m
            p = jnp.exp(s - m_new)
            l = jnp.sum(p, axis=-1, keepdims=True)
            acc = _dot(p.astype(BF16), vv)
            if has_in:
                alpha = jnp.exp(m0 - m_new)
                l = alpha * l0 + l
                acc = alpha * ai_ref[rows, hs] + acc
            if final:
                o_ref[rows, hs] = (acc / l).astype(o_ref.dtype)
            else:
                ao_ref[rows, hs] = acc
                m_tile = jnp.where(lane == h, m_new, m_tile)
                l_tile = jnp.where(lane == h, l, l_tile)
        if not final:
            mo_ref[rows, :] = m_tile
            lo_ref[rows, :] = l_tile


def _band_attn(q, k, v, state, dil, final):
    B, S, _ = q.shape
    L = S // dil
    tq = min(512, L)
    sub = tq // BLK
    view = lambda t, c: t.reshape(B, L, dil * c)
    own = lambda c: pl.BlockSpec((None, tq, c), lambda b, r, i: (b, i, r))
    prev = pl.BlockSpec((None, BLK, ATTN_W), lambda b, r, i: (b, jnp.maximum(i * sub - 1, 0), r))
    has_in = state is not None
    ins = [view(q, ATTN_W), view(k, ATTN_W), view(k, ATTN_W), view(v, ATTN_W), view(v, ATTN_W)]
    in_specs = [own(ATTN_W), own(ATTN_W), prev, own(ATTN_W), prev]
    state_specs = [own(ML_LANES), own(ML_LANES), own(ATTN_W)]
    if has_in:
        m, l, acc = state
        ins += [view(m, ML_LANES), view(l, ML_LANES), view(acc, ATTN_W)]
        in_specs += state_specs
    if final:
        out_specs = own(ATTN_W)
        out_shape = jax.ShapeDtypeStruct((B, L, dil * ATTN_W), BF16)
    else:
        out_specs = state_specs
        out_shape = [jax.ShapeDtypeStruct((B, L, dil * ML_LANES), F32)] * 2 + [
            jax.ShapeDtypeStruct((B, L, dil * ATTN_W), F32)]
    out = pl.pallas_call(
        functools.partial(_band_attn_kernel, tq=tq, has_in=has_in, final=final),
        grid=(B, dil, L // tq),
        in_specs=in_specs, out_specs=out_specs, out_shape=out_shape,
        compiler_params=_params(("arbitrary",) * 3),
        name=f"p_attn_d{dil}",
    )(*ins)
    if final:
        return out.reshape(B, S, ATTN_W)
    m, l, acc = out
    return m.reshape(B, S, ML_LANES), l.reshape(B, S, ML_LANES), acc.reshape(B, S, ATTN_W)


HALO = 32
CONV_ROWS = 64


def _p_mix_kernel(attn_ref, u_ref, uh_ref, x_ref, cw_ref, cb_ref, lng_ref, lnb_ref, goa_ref, goc_ref,
                  wout_ref, gpm_ref, o_ref, ext_ref, c_ref, *, tm):
    j = pl.program_id(1)
    ext_ref[0:HALO, :] = jnp.where(j > 0, uh_ref[...], 0.0)
    ext_ref[HALO:HALO + tm, :] = u_ref[...]
    off = HALO - (CONV_K - 1)
    for r0 in range(0, tm, CONV_ROWS):
        acc = jnp.zeros((CONV_ROWS, CONV_CH), F32)
        for t in range(CONV_K):
            acc = acc + cw_ref[t:t + 1, :] * ext_ref[r0 + off + t:r0 + off + t + CONV_ROWS, :]
        c_ref[r0:r0 + CONV_ROWS, :] = acc
    c = _conv_post(c_ref[...] + cb_ref[...], lng_ref[...], lnb_ref[...])
    y = _mix_out(attn_ref[...].astype(F32), c, goa_ref[...], goc_ref[...], wout_ref)
    o_ref[...] = x_ref[...] + _rms(y, gpm_ref[...])


def _p_mix(attn, u, x, cw, cb, lng, lnb, goa, goc, wout, gpm, tm):
    B, S, _ = x.shape
    tok = lambda c: pl.BlockSpec((None, tm, c), lambda b, j: (b, j, 0))
    halo = pl.BlockSpec((None, HALO, CONV_CH), lambda b, j: (b, jnp.maximum(j * (tm // HALO) - 1, 0), 0))
    return pl.pallas_call(
        functools.partial(_p_mix_kernel, tm=tm),
        grid=(B, S // tm),
        in_specs=[tok(ATTN_W), tok(CONV_CH), halo, tok(D_MODEL),
                  _const_spec(cw.shape), _const_spec(cb.shape), _const_spec(lng.shape), _const_spec(lnb.shape),
                  _const_spec(goa.shape), _const_spec(goc.shape), _const_spec(wout.shape), _const_spec(gpm.shape)],
        out_specs=tok(D_MODEL),
        out_shape=jax.ShapeDtypeStruct((B, S, D_MODEL), F32),
        scratch_shapes=[pltpu.VMEM((HALO + tm, CONV_CH), F32), pltpu.VMEM((tm, CONV_CH), F32)],
        compiler_params=_params(("arbitrary", "arbitrary")),
        name="p_mix",
    )(attn, u, u, x, cw, cb, lng, lnb, goa, goc, wout, gpm)


def _p_memkv_kernel(mem_ref, g_ref, wk_ref, wv_ref, k_ref, v_ref):
    h = _rms(mem_ref[...], g_ref[...]).astype(BF16)
    k_ref[...] = _dot(h, wk_ref[...])
    v_ref[...] = _dot(h, wv_ref[...])


def _p_memkv(mem, g, wk, wv):
    B, M, _ = mem.shape
    blk = pl.BlockSpec((None, M, D_MODEL), lambda b: (b, 0, 0))
    return pl.pallas_call(
        _p_memkv_kernel,
        grid=(B,),
        in_specs=[blk, _const_spec(g.shape), _const_spec(wk.shape), _const_spec(wv.shape)],
        out_specs=[blk, blk],
        out_shape=[jax.ShapeDtypeStruct((B, M, D_MODEL), F32)] * 2,
        compiler_params=_params(("arbitrary",)),
        name="p_memkv",
    )(mem, g, wk, wv)


def _xattn_heads(qx, mk, mv):
    outs = []
    for h in range(N_XHEADS):
        hs = slice(h * XHEAD_DIM, (h + 1) * XHEAD_DIM)
        s = _dot_nt(qx[:, hs], mk[:, hs])
        m = jnp.max(s, axis=-1, keepdims=True)
        p = jnp.exp(s - m)
        l = jnp.sum(p, axis=-1, keepdims=True)
        outs.append(_dot((p / l).astype(BF16), mv[:, hs]).astype(BF16))
    return jnp.concatenate(outs, axis=-1)


def _p_xattn_kernel(x_ref, mk_ref, mv_ref, gpre_ref, wq_ref, wo_ref, gpost_ref, o_ref):
    x = x_ref[...]
    h = _rms(x, gpre_ref[...]).astype(BF16)
    qx = (_dot(h, wq_ref[...]) * (XHEAD_DIM ** -0.5)).astype(BF16)
    ox = _xattn_heads(qx, mk_ref[...].astype(BF16), mv_ref[...].astype(BF16))
    o_ref[...] = x + _rms(_dot(ox, wo_ref[...]), gpost_ref[...])


def _p_xattn(x, mk, mv, gpre, wq, wo, gpost, tm):
    B, S, _ = x.shape
    tok = pl.BlockSpec((None, tm, D_MODEL), lambda b, j: (b, j, 0))
    mem = pl.BlockSpec((None, N_MEM, D_MODEL), lambda b, j: (b, 0, 0))
    return pl.pallas_call(
        _p_xattn_kernel,
        grid=(B, S // tm),
        in_specs=[tok, mem, mem, _const_spec(gpre.shape), _const_spec(wq.shape), _const_spec(wo.shape),
                  _const_spec(gpost.shape)],
        out_specs=tok,
        out_shape=jax.ShapeDtypeStruct((B, S, D_MODEL), F32),
        compiler_params=_params(("arbitrary", "arbitrary")),
        name="p_xattn",
    )(x, mk, mv, gpre, wq, wo, gpost)


FFN_PAD = 8


def _p_ffn_kernel(x_ref, gpre_ref, wup_ref, cw_ref, wdn_ref, gpost_ref, y_ref, st_ref, ext_ref, g_ref, *, tm):
    j = pl.program_id(1)

    @pl.when(j == 0)
    def _():
        ext_ref[0:FFN_PAD, :] = jnp.zeros((FFN_PAD, D_FF), F32)

    x = x_ref[...]
    h = _rms(x, gpre_ref[...]).astype(BF16)
    for c0 in range(0, D_FF, FF_CHUNK):
        cs = slice(c0, c0 + FF_CHUNK)
        ext_ref[FFN_PAD:FFN_PAD + tm, cs] = _dot(h, wup_ref[:, cs])
        lin = _dot(h, wup_ref[:, D_FF + c0:D_FF + c0 + FF_CHUNK])
        f = (cw_ref[0:1, cs] * ext_ref[FFN_PAD - 2:FFN_PAD - 2 + tm, cs]
             + cw_ref[1:2, cs] * ext_ref[FFN_PAD - 1:FFN_PAD - 1 + tm, cs]
             + cw_ref[2:3, cs] * ext_ref[FFN_PAD:FFN_PAD + tm, cs])
        g_ref[:, cs] = _ffn_act(f, lin)
    tail = ext_ref[tm:tm + FFN_PAD, :]
    ext_ref[0:FFN_PAD, :] = tail
    st_ref[...] = tail[FFN_PAD - (FFN_K - 1):FFN_PAD, :]
    y_ref[...] = x + _rms(_dot(g_ref[...], wdn_ref[...]), gpost_ref[...])


def _p_ffn(x, gpre, wup, cw, wdn, gpost, tm):
    B, S, _ = x.shape
    tok = pl.BlockSpec((None, tm, D_MODEL), lambda b, j: (b, j, 0))
    st = pl.BlockSpec((None, FFN_K - 1, D_FF), lambda b, j: (b, 0, 0))
    return pl.pallas_call(
        functools.partial(_p_ffn_kernel, tm=tm),
        grid=(B, S // tm),
        in_specs=[tok, _const_spec(gpre.shape), _const_spec(wup.shape), _const_spec(cw.shape),
                  _const_spec(wdn.shape), _const_spec(gpost.shape)],
        out_specs=[tok, st],
        out_shape=[jax.ShapeDtypeStruct((B, S, D_MODEL), F32), jax.ShapeDtypeStruct((B, FFN_K - 1, D_FF), F32)],
        scratch_shapes=[pltpu.VMEM((FFN_PAD + tm, D_FF), F32), pltpu.VMEM((tm, D_FF), BF16)],
        compiler_params=_params(("arbitrary", "arbitrary")),
        name="p_ffn",
    )(x, gpre, wup, cw, wdn, gpost)


def _tcol(t, c):
    return slice(t * c, (t + 1) * c)


def _stack_t(ref, c, fn):
    return jnp.concatenate([fn(ref[:, _tcol(t, c)]) for t in range(T_NEW)], axis=0)


def _s_inproj_kernel(x_ref, g_ref, w_ref, q_ref, k_ref, v_ref, u_ref):
    bd = x_ref.shape[0]
    g = g_ref[...]
    h = _stack_t(x_ref, D_MODEL, lambda xt: _rms(xt, g).astype(BF16))
    q, k, v, u = _in_proj(h, w_ref)
    for t in range(T_NEW):
        rows = slice(t * bd, (t + 1) * bd)
        q_ref[:, _tcol(t, ATTN_W)] = q[rows]
        k_ref[:, _tcol(t, ATTN_W)] = k[rows]
        v_ref[:, _tcol(t, ATTN_W)] = v[rows]
        u_ref[:, _tcol(t, CONV_CH)] = u[rows]


def _single_step_call(kernel, ins, out_shapes, name, scratch=()):
    return pl.pallas_call(
        kernel,
        grid=(1,),
        in_specs=[_const_spec(a.shape) for a in ins],
        out_specs=[pl.BlockSpec(s.shape, lambda *_, nd=len(s.shape): (0,) * nd) for s in out_shapes],
        out_shape=out_shapes,
        scratch_shapes=list(scratch),
        compiler_params=_params(("arbitrary",)),
        name=name,
    )(*ins)


def _s_inproj(x, g, w_in):
    bd = x.shape[0]
    shp = jax.ShapeDtypeStruct((bd, T_NEW * ATTN_W), F32)
    return _single_step_call(_s_inproj_kernel, [x, g, w_in], [shp] * 4, "s_inproj")


SB = 8
WIN_VIEW_ROWS = MAX_WINDOW // 16
SLAB_SPLIT = WIN_VIEW_ROWS - 512 // 16
P_D16, P_D4, P_D1, P_NEW = 0, 128, 256, 384
P_ROWS = P_NEW + 8


def _s_attn_kernel(q_ref, kn_ref, vn_ref, k1_ref, k2_ref, v1_ref, v2_ref, e_ref, et_ref, o_ref, prod_ref, pe_ref):
    ii = pl.program_id(1)
    rows8 = lax.broadcasted_iota(jnp.int32, (SB, ATTN_W), 0)
    sel = rows8 == ii

    @pl.when(ii == 0)
    def _():
        o_ref[...] = jnp.zeros(o_ref.shape, F32)

    def pick(ref, t):
        return jnp.sum(jnp.where(sel, ref[:, _tcol(t, ATTN_W)], 0.0), axis=0, keepdims=True)

    n2 = WIN_VIEW_ROWS - SLAB_SPLIT
    prow = lax.broadcasted_iota(jnp.int32, (P_ROWS, ML_LANES), 0)
    for t in range(T_NEW):
        q = pick(q_ref, t)
        pieces = [(P_D16, lambda r=k1_ref, t=t: r[:, _tcol(t, ATTN_W)], lambda r=v1_ref, t=t: r[:, _tcol(t, ATTN_W)],
                   SLAB_SPLIT),
                  (P_D16 + SLAB_SPLIT, lambda r=k2_ref, t=t: r[:, _tcol(t, ATTN_W)],
                   lambda r=v2_ref, t=t: r[:, _tcol(t, ATTN_W)], n2)]
        for e in range(4):
            c = t + 4 * e
            pieces.append((P_D4 + n2 * e, lambda r=k2_ref, c=c: r[:, _tcol(c, ATTN_W)],
                           lambda r=v2_ref, c=c: r[:, _tcol(c, ATTN_W)], n2))
        for c in range(16):
            pieces.append((P_D1 + 8 * c, lambda r=k2_ref, c=c: r[n2 - 8:n2, _tcol(c, ATTN_W)],
                           lambda r=v2_ref, c=c: r[n2 - 8:n2, _tcol(c, ATTN_W)], 8))
        for off, kload, _, n in pieces:
            prod_ref[off:off + n, :] = kload() * q
        new_k = [pick(kn_ref, tp) for tp in range(t + 1)]
        new_v = [pick(vn_ref, tp) for tp in range(t + 1)]
        prod_ref[P_NEW:P_NEW + 8, :] = jnp.zeros((8, ATTN_W), F32)
        for tp in range(t + 1):
            prod_ref[P_NEW + tp:P_NEW + tp + 1, :] = new_k[tp] * q
        hi, lo = _split_bf16(prod_ref[...])
        s = _dot(hi, e_ref[...]) + _dot(lo, e_ref[...])
        d1_row = prow - P_D1
        d1_dead = (d1_row >= 0) & (d1_row < 8 * t) & (d1_row % 8 == 0)
        mult = jnp.where(prow < P_NEW, jnp.where(d1_dead, 0.0, 1.0),
                         jnp.where(prow < P_NEW + t, 1.0, jnp.where(prow == P_NEW + t, 3.0, 0.0)))
        s = jnp.where(mult > 0.0, s, NEG)
        m = jnp.max(s, axis=0, keepdims=True)
        p = jnp.exp(s - m) * mult
        pn = p / jnp.sum(p, axis=0, keepdims=True)
        hi, lo = _split_bf16(pn)
        pe_ref[...] = _dot(hi, et_ref[...]) + _dot(lo, et_ref[...])
        o = jnp.zeros((1, ATTN_W), F32)
        for off, _, vload, n in pieces:
            o = o + jnp.sum(pe_ref[off:off + n, :] * vload(), axis=0, keepdims=True)
        for tp in range(t + 1):
            o = o + pe_ref[P_NEW + tp:P_NEW + tp + 1, :] * new_v[tp]
        cur = o_ref[:, _tcol(t, ATTN_W)]
        o_ref[:, _tcol(t, ATTN_W)] = jnp.where(sel, o, cur)


def _s_attn(q, kn, vn, win_k, win_v):
    bd = q.shape[0]
    kview = win_k.reshape(bd, WIN_VIEW_ROWS, 16 * ATTN_W)
    vview = win_v.reshape(bd, WIN_VIEW_ROWS, 16 * ATTN_W)
    n2 = WIN_VIEW_ROWS - SLAB_SPLIT
    assert SLAB_SPLIT % n2 == 0
    head_of_col = jnp.arange(ATTN_W) // HEAD_DIM
    e = (head_of_col[:, None] == jnp.arange(ML_LANES)[None, :]).astype(BF16)
    rowblk = pl.BlockSpec((SB, T_NEW * ATTN_W), lambda g, i: (g, 0))
    slab1 = pl.BlockSpec((None, SLAB_SPLIT, T_NEW * ATTN_W), lambda g, i: (g * SB + i, 0, 0))
    slab2 = pl.BlockSpec((None, n2, 16 * ATTN_W), lambda g, i: (g * SB + i, SLAB_SPLIT // n2, 0))
    return pl.pallas_call(
        _s_attn_kernel,
        grid=(bd // SB, SB),
        in_specs=[rowblk, rowblk, rowblk, slab1, slab2, slab1, slab2, _const_spec(e.shape), _const_spec(e.T.shape)],
        out_specs=rowblk,
        out_shape=jax.ShapeDtypeStruct((bd, T_NEW * ATTN_W), F32),
        scratch_shapes=[pltpu.VMEM((P_ROWS, ATTN_W), F32), pltpu.VMEM((P_ROWS, ATTN_W), F32)],
        compiler_params=_params(("arbitrary", "arbitrary")),
        name="s_attn",
    )(q, kn, vn, kview, kview, vview, vview, e, e.T)


def _s_mix_kernel(attn_ref, u_ref, st_ref, x_ref, cw_ref, cb_ref, lng_ref, lnb_ref, goa_ref, goc_ref,
                  wout_ref, gpm_ref, o_ref, nst_ref):
    bd = x_ref.shape[0]
    nctx = CONV_K - 1

    def ctx(r):
        return st_ref[:, _tcol(r, CONV_CH)] if r < nctx else u_ref[:, _tcol(r - nctx, CONV_CH)]

    cs = []
    for t in range(T_NEW):
        acc = jnp.zeros((bd, CONV_CH), F32)
        for j in range(CONV_K):
            acc = acc + cw_ref[j:j + 1, :] * ctx(t + j)
        cs.append(_conv_post(acc + cb_ref[...], lng_ref[...], lnb_ref[...]))
    c = jnp.concatenate(cs, axis=0)
    attn = _stack_t(attn_ref, ATTN_W, lambda a: a)
    y = _mix_out(attn, c, goa_ref[...], goc_ref[...], wout_ref)
    for t in range(T_NEW):
        o_ref[:, _tcol(t, D_MODEL)] = x_ref[:, _tcol(t, D_MODEL)] + _rms(y[t * bd:(t + 1) * bd], gpm_ref[...])
    for r in range(nctx):
        nst_ref[:, _tcol(r, CONV_CH)] = ctx(r + T_NEW)


def _s_mix(attn, u, state, x, cw, cb, lng, lnb, goa, goc, wout, gpm):
    bd = x.shape[0]
    outs = [jax.ShapeDtypeStruct((bd, T_NEW * D_MODEL), F32), jax.ShapeDtypeStruct(state.shape, F32)]
    return _single_step_call(_s_mix_kernel, [attn, u, state, x, cw, cb, lng, lnb, goa, goc, wout, gpm], outs, "s_mix")


def _s_xq_kernel(x_ref, g_ref, wq_ref, q_ref):
    bd = x_ref.shape[0]
    g = g_ref[...]
    h = _stack_t(x_ref, D_MODEL, lambda xt: _rms(xt, g).astype(BF16))
    q = _dot(h, wq_ref[...]) * (XHEAD_DIM ** -0.5)
    for t in range(T_NEW):
        q_ref[:, _tcol(t, D_MODEL)] = q[t * bd:(t + 1) * bd]


def _s_xcore_kernel(q_ref, mk_ref, mv_ref, o_ref):
    ii = pl.program_id(1)
    rows8 = lax.broadcasted_iota(jnp.int32, (SB, D_MODEL), 0)
    sel = rows8 == ii

    @pl.when(ii == 0)
    def _():
        o_ref[...] = jnp.zeros(o_ref.shape, F32)

    qx = jnp.zeros((SB, D_MODEL), F32)
    for t in range(T_NEW):
        qt = jnp.sum(jnp.where(sel, q_ref[:, _tcol(t, D_MODEL)], 0.0), axis=0, keepdims=True)
        qx = jnp.where(rows8 == t, qt, qx)
    ox = _xattn_heads(qx.astype(BF16), mk_ref[...].astype(BF16), mv_ref[...].astype(BF16)).astype(F32)
    for t in range(T_NEW):
        cur = o_ref[:, _tcol(t, D_MODEL)]
        o_ref[:, _tcol(t, D_MODEL)] = jnp.where(sel, ox[t:t + 1, :], cur)


def _s_xo_kernel(ox_ref, x_ref, wo_ref, g_ref, o_ref):
    bd = x_ref.shape[0]
    ox = _stack_t(ox_ref, D_MODEL, lambda a: a.astype(BF16))
    y = _dot(ox, wo_ref[...])
    for t in range(T_NEW):
        o_ref[:, _tcol(t, D_MODEL)] = x_ref[:, _tcol(t, D_MODEL)] + _rms(y[t * bd:(t + 1) * bd], g_ref[...])


def _s_xattn(x, mem_k, mem_v, gpre, wq, wo, gpost):
    bd = x.shape[0]
    full = jax.ShapeDtypeStruct((bd, T_NEW * D_MODEL), F32)
    (q,) = _single_step_call(_s_xq_kernel, [x, gpre, wq], [full], "s_xq")
    rowblk = pl.BlockSpec((SB, T_NEW * D_MODEL), lambda g, i: (g, 0))
    mem = pl.BlockSpec((None, N_MEM, D_MODEL), lambda g, i: (g * SB + i, 0, 0))
    ox = pl.pallas_call(
        _s_xcore_kernel,
        grid=(bd // SB, SB),
        in_specs=[rowblk, mem, mem],
        out_specs=rowblk,
        out_shape=full,
        compiler_params=_params(("arbitrary", "arbitrary")),
        name="s_xcore",
    )(q, mem_k, mem_v)
    (out,) = _single_step_call(_s_xo_kernel, [ox, x, wo, gpost], [full], "s_xo")
    return out


def _s_ffn_kernel(x_ref, st_ref, gpre_ref, wup_ref, cw_ref, wdn_ref, gpost_ref, y_ref, nst_ref, g_ref):
    bd = x_ref.shape[0]
    gpre = gpre_ref[...]
    h = _stack_t(x_ref, D_MODEL, lambda xt: _rms(xt, gpre).astype(BF16))
    for c0 in range(0, D_FF, FF_CHUNK):
        cs = slice(c0, c0 + FF_CHUNK)
        act = _dot(h, wup_ref[:, cs])
        lin = _dot(h, wup_ref[:, D_FF + c0:D_FF + c0 + FF_CHUNK])
        a = [st_ref[:, c0 + r * D_FF:c0 + r * D_FF + FF_CHUNK] for r in range(FFN_K - 1)]
        a += [act[t * bd:(t + 1) * bd] for t in range(T_NEW)]
        for t in range(T_NEW):
            f = cw_ref[0:1, cs] * a[t] + cw_ref[1:2, cs] * a[t + 1] + cw_ref[2:3, cs] * a[t + 2]
            g_ref[t * bd:(t + 1) * bd, cs] = _ffn_act(f, lin[t * bd:(t + 1) * bd])
        for r in range(FFN_K - 1):
            nst_ref[:, c0 + r * D_FF:c0 + r * D_FF + FF_CHUNK] = a[T_NEW + r]
    y = _dot(g_ref[...], wdn_ref[...])
    for t in range(T_NEW):
        y_ref[:, _tcol(t, D_MODEL)] = x_ref[:, _tcol(t, D_MODEL)] + _rms(y[t * bd:(t + 1) * bd], gpost_ref[...])


def _s_ffn(x, state, gpre, wup, cw, wdn, gpost):
    bd = x.shape[0]
    outs = [jax.ShapeDtypeStruct((bd, T_NEW * D_MODEL), F32), jax.ShapeDtypeStruct(state.shape, F32)]
    return _single_step_call(_s_ffn_kernel, [x, state, gpre, wup, cw, wdn, gpost], outs, "s_ffn",
                             scratch=[pltpu.VMEM((T_NEW * bd, D_FF), BF16)])


def kernel(x_prompt, x_sample, mem_prompt, cache_win_k, cache_win_v, state_conv, state_ffn_conv, cache_mem_k, cache_mem_v, g_pre_mix, w_in, conv_w, conv_b, ln_conv_g, ln_conv_b, g_out_attn, g_out_conv, w_out, g_post_mix, g_pre_x, g_mem, w_xq, w_mk, w_mv, w_xo, g_post_x, g_pre_ffn, w_up, ffn_conv_w, w_down, g_post_ffn):
    depth = w_in.shape[0]
    assert depth == 1
    B, S, _ = x_prompt.shape
    Bd, T, _ = x_sample.shape
    assert T == T_NEW and cache_win_k.shape[2] == MAX_WINDOW and S % (16 * BLK) == 0 and Bd % SB == 0

    gpm, gpx, gm, gqx, gpf, gqf = (_row(a[0]) for a in (g_pre_mix, g_pre_x, g_mem, g_post_x, g_pre_ffn, g_post_ffn))
    gpo, goa, goc = _row(g_post_mix[0]), _row(g_out_attn[0]), _row(g_out_conv[0])
    cb, lng, lnb = _row(conv_b[0]), _row(ln_conv_g[0]), _row(ln_conv_b[0])
    cw, fcw = conv_w[0], ffn_conv_w[0]
    win, wout, wxq, wmk, wmv, wxo, wup, wdn = (
        a[0].astype(BF16) for a in (w_in, w_out, w_xq, w_mk, w_mv, w_xo, w_up, w_down))

    q, k, v, win_k_p, win_v_p, u = _p_inproj(x_prompt, gpm, win, tm=512)
    state = None
    for dil in DILATIONS[:-1]:
        state = _band_attn(q, k, v, state, dil, final=False)
    attn = _band_attn(q, k, v, state, DILATIONS[-1], final=True)
    x1 = _p_mix(attn, u, x_prompt, cw, cb, lng, lnb, goa, goc, wout, gpo, tm=512)
    mk_p, mv_p = _p_memkv(mem_prompt, gm, wmk, wmv)
    x2 = _p_xattn(x1, mk_p, mv_p, gpx, wxq, wxo, gqx, tm=512)
    y_p, ffn_p = _p_ffn(x2, gpf, wup, fcw, wdn, gqf, tm=256)
    conv_p = u[:, S - (CONV_K - 1):, :]

    xs = x_sample.reshape(Bd, T * D_MODEL)
    qs, ks, vs, us = _s_inproj(xs, gpm, win)
    attn_s = _s_attn(qs, ks, vs, cache_win_k[0], cache_win_v[0])
    x1s, conv_s = _s_mix(attn_s, us, state_conv[0].reshape(Bd, -1), xs, cw, cb, lng, lnb, goa, goc, wout, gpo)
    x2s = _s_xattn(x1s, cache_mem_k[0].reshape(Bd, N_MEM, D_MODEL), cache_mem_v[0].reshape(Bd, N_MEM, D_MODEL),
                   gpx, wxq, wxo, gqx)
    y_s, ffn_s = _s_ffn(x2s, state_ffn_conv[0].reshape(Bd, -1), gpf, wup, fcw, wdn, gqf)

    W = min(MAX_WINDOW, S)
    return (y_p, y_s.reshape(Bd, T, D_MODEL),
            win_k_p.reshape(1, B, W, N_HEADS, HEAD_DIM), win_v_p.reshape(1, B, W, N_HEADS, HEAD_DIM),
            conv_p[None], ffn_p[None],
            mk_p.reshape(1, B, N_MEM, N_XHEADS, XHEAD_DIM), mv_p.reshape(1, B, N_MEM, N_XHEADS, XHEAD_DIM),
            ks.reshape(1, Bd, T, N_HEADS, HEAD_DIM), vs.reshape(1, Bd, T, N_HEADS, HEAD_DIM),
            conv_s.reshape(1, Bd, CONV_K - 1, CONV_CH), ffn_s.reshape(1, Bd, FFN_K - 1, D_FF))
```

```python
import functools
import math

import jax
import jax.numpy as jnp
import numpy as np
from jax import lax
from jax.experimental import pallas as pl
from jax.experimental.pallas import tpu as pltpu

F32 = jnp.float32
BF16 = jnp.bfloat16

EPS = 1e-6
D_MODEL = 1024
N_HEADS = 8
HEAD_DIM = 64
ATTN_W = N_HEADS * HEAD_DIM
CONV_CH = D_MODEL - ATTN_W
CONV_K = 31
BLK = 128
DILATIONS = (16, 4, 1)
MAX_WINDOW = 2048
N_MEM = 256
N_XHEADS = 4
XHEAD_DIM = D_MODEL // N_XHEADS
D_FF = 2816
FFN_K = 3
T_NEW = 4
NEG = -1e30
LANES = 128
N_SLABS = ATTN_W // LANES
FF_CHUNK = 256
VMEM_LIMIT = 56 * 1024 * 1024

CHUNK = 16 * BLK
PHASES_PER_BRANCH = 4
TQ = CHUNK // PHASES_PER_BRANCH


def _rms(x, g):
    return x * lax.rsqrt(jnp.mean(x * x, axis=-1, keepdims=True) + EPS) * g


def _dot(a, b):
    return jnp.dot(a, b, preferred_element_type=F32)


def _dot_nt(a, b):
    return lax.dot_general(a, b, (((1,), (1,)), ((), ())), preferred_element_type=F32)


def _sigmoid(x):
    return 1.0 / (1.0 + jnp.exp(-x))


def _gelu_tanh(x):
    c = math.sqrt(2.0 / math.pi)
    return x * (0.5 * (1.0 + jnp.tanh(c * (x + 0.044715 * (x * x * x)))))


def _conv_post(c, lng, lnb):
    mu = jnp.mean(c, axis=-1, keepdims=True)
    d = c - mu
    var = jnp.mean(d * d, axis=-1, keepdims=True)
    y = d * lax.rsqrt(var + EPS) * lng + lnb
    return y * _sigmoid(y)


def _ffn_act(f, lin):
    return (_gelu_tanh(f) * lin).astype(BF16)


def _xattn_heads(qx, mk_ref, mv_ref):
    outs = []
    for h in range(N_XHEADS):
        hs = slice(h * XHEAD_DIM, (h + 1) * XHEAD_DIM)
        s = _dot_nt(qx[:, hs], mk_ref[:, h, :].astype(BF16))
        m = jnp.max(s, axis=-1, keepdims=True)
        p = jnp.exp(s - m)
        l = jnp.sum(p, axis=-1, keepdims=True)
        outs.append(_dot((p / l).astype(BF16), mv_ref[:, h, :].astype(BF16)).astype(BF16))
    return jnp.concatenate(outs, axis=-1)


def _const_spec(shape):
    nd = len(shape)
    return pl.BlockSpec(shape, lambda *_: (0,) * nd, pipeline_mode=pl.Buffered(1))


def _params(sem):
    return pltpu.CompilerParams(dimension_semantics=sem, vmem_limit_bytes=VMEM_LIMIT)


def _row(v):
    return v.reshape(1, -1).astype(F32)


def _clip(v, hi):
    return jnp.minimum(jnp.maximum(v, 0), hi)


HALO = 32
CONV_ROWS = 32


def _p_inproj_kernel(x_ref, g_ref, w_ref, cw_ref, cb_ref, lng_ref, lnb_ref, goc_ref,
                     q1_ref, k1_ref, v1_ref, q4_ref, k4_ref, v4_ref, q16_ref, k16_ref, v16_ref,
                     kt_ref, vt_ref, mc_ref, cst_ref, zs_ref, ext_ref, c_ref, *, tm, first_win_tile):
    j = pl.program_id(1)
    h = _rms(x_ref[...], g_ref[...]).astype(BF16)

    def emit(z, tok_ref, d4_ref, d16_ref):
        tok_ref[...] = z.astype(BF16)
        for s in range(N_SLABS):
            zs_ref[s] = z[:, s * LANES:(s + 1) * LANES]
        for s in range(N_SLABS):
            ls = slice(s * LANES, (s + 1) * LANES)
            for r in range(4):
                d4_ref[r, :, ls] = zs_ref[s, pl.ds(r, tm // 4, stride=4), :].astype(BF16)
            for r in range(16):
                d16_ref[r, :, ls] = zs_ref[s, pl.ds(r, tm // 16, stride=16), :].astype(BF16)

    emit(_dot(h, w_ref[:, 0:ATTN_W]) * (HEAD_DIM ** -0.5), q1_ref, q4_ref, q16_ref)
    k = _dot(h, w_ref[:, ATTN_W:2 * ATTN_W])
    emit(k, k1_ref, k4_ref, k16_ref)
    v = _dot(h, w_ref[:, 2 * ATTN_W:3 * ATTN_W])
    emit(v, v1_ref, v4_ref, v16_ref)

    @pl.when(j >= first_win_tile)
    def _():
        kt_ref[...] = k.T
        vt_ref[...] = v.T

    a = _dot(h, w_ref[:, 3 * ATTN_W:3 * ATTN_W + CONV_CH])
    gate = _dot(h, w_ref[:, 3 * ATTN_W + CONV_CH:3 * ATTN_W + 2 * CONV_CH])
    u = a * _sigmoid(gate)
    cst_ref[...] = u[tm - HALO:tm, :]

    @pl.when(j == 0)
    def _():
        ext_ref[:, 0:HALO, :] = jnp.zeros((N_SLABS, HALO, LANES), F32)

    @pl.when(j > 0)
    def _():
        ext_ref[:, 0:HALO, :] = ext_ref[:, tm:tm + HALO, :]

    for s in range(N_SLABS):
        ext_ref[s, HALO:HALO + tm, :] = u[:, s * LANES:(s + 1) * LANES]
    off = HALO - (CONV_K - 1)
    for s in range(N_SLABS):
        ls = slice(s * LANES, (s + 1) * LANES)
        for r0 in range(0, tm, CONV_ROWS):
            acc = jnp.zeros((CONV_ROWS, LANES), F32)
            for t in range(CONV_K):
                acc = acc + cw_ref[t:t + 1, ls] * ext_ref[s, r0 + off + t:r0 + off + t + CONV_ROWS, :]
            c_ref[r0:r0 + CONV_ROWS, ls] = acc
    c = _conv_post(c_ref[...] + cb_ref[...], lng_ref[...], lnb_ref[...])
    mc_ref[...] = _rms(c, goc_ref[...]).astype(BF16)


def _p_inproj(x, g, w_in, cw, cb, lng, lnb, goc, tm):
    B, S, _ = x.shape
    W = min(MAX_WINDOW, S)
    nt = S // tm
    first = (S - W) // tm
    tok = lambda c: pl.BlockSpec((None, tm, c), lambda b, j: (b, j, 0))
    cls = lambda d: pl.BlockSpec((None, d, tm // d, ATTN_W), lambda b, j: (b, 0, j, 0))
    win = pl.BlockSpec((None, ATTN_W, tm), lambda b, j: (b, 0, jnp.maximum(j - first, 0)))
    cst = pl.BlockSpec((None, HALO, CONV_CH), lambda b, j: (b, 0, 0))
    bf = lambda *shape: jax.ShapeDtypeStruct(shape, BF16)
    consts = [g, w_in, cw, cb, lng, lnb, goc]
    return pl.pallas_call(
        functools.partial(_p_inproj_kernel, tm=tm, first_win_tile=first),
        grid=(B, nt),
        in_specs=[tok(D_MODEL)] + [_const_spec(a.shape) for a in consts],
        out_specs=[tok(ATTN_W)] * 3 + [cls(4)] * 3 + [cls(16)] * 3 + [win, win, tok(CONV_CH), cst],
        out_shape=[bf(B, S, ATTN_W)] * 3 + [bf(B, 4, S // 4, ATTN_W)] * 3 + [bf(B, 16, S // 16, ATTN_W)] * 3
        + [jax.ShapeDtypeStruct((B, ATTN_W, W), F32)] * 2
        + [bf(B, S, CONV_CH), jax.ShapeDtypeStruct((B, HALO, CONV_CH), F32)],
        scratch_shapes=[pltpu.VMEM((N_SLABS, tm, LANES), F32), pltpu.VMEM((N_SLABS, HALO + tm, LANES), F32),
                        pltpu.VMEM((tm, CONV_CH), F32)],
        compiler_params=_params(("arbitrary", "arbitrary")),
        name="p_inproj",
    )(x, *consts)


def _pair_scores(qp, kk, vv, bias):
    lo = lax.broadcasted_iota(jnp.int32, (BLK, LANES), 1) < HEAD_DIM
    lo2 = lax.broadcasted_iota(jnp.int32, (2 * BLK, LANES), 1) < HEAD_DIM
    zq = jnp.zeros_like(qp)
    zv = jnp.zeros_like(vv)
    q2 = jnp.concatenate([jnp.where(lo, qp, zq), jnp.where(lo, zq, qp)], axis=0)
    s = _dot_nt(q2, kk) + bias
    m = jnp.max(s, axis=-1, keepdims=True)
    p = jnp.exp(s - m)
    l = jnp.sum(p, axis=-1, keepdims=True)
    pb = p.astype(BF16)
    pcat = jnp.concatenate([pb[0:BLK], pb[BLK:2 * BLK]], axis=1)
    vcat = jnp.concatenate([jnp.where(lo2, vv, zv), jnp.where(lo2, zv, vv)], axis=0)
    pv = _dot(pcat, vcat)
    m_pair = jnp.where(lo, m[0:BLK], m[BLK:2 * BLK])
    l_pair = jnp.where(lo, l[0:BLK], l[BLK:2 * BLK])
    return m_pair, l_pair, pv


def _p_attn_kernel(q16_ref, k16_ref, k16p_ref, v16_ref, v16p_ref,
                   q4_ref, k4_ref, k4p_ref, v4_ref, v4p_ref,
                   q1_ref, k1_ref, k1p_ref, v1_ref, v1p_ref,
                   o_ref, m_ref, l_ref, a_ref):
    c = pl.program_id(1)
    ph = pl.program_id(2)
    row = lax.broadcasted_iota(jnp.int32, (2 * BLK, 2 * BLK), 0) % BLK
    col = lax.broadcasted_iota(jnp.int32, (2 * BLK, 2 * BLK), 1)
    band = (col >= row) & (col <= row + BLK)
    bias_band = jnp.where(band, 0.0, NEG)
    bias_own = jnp.where(band & (col >= BLK), 0.0, NEG)

    def merge(state_rows, fresh, first, final, out_rows=None):
        pair, rows = state_rows
        m_b, l_b, pv = fresh
        if first:
            m_new, l_new, a_new = m_b, l_b, pv
        else:
            m0 = m_ref[pair, rows, :]
            m_new = jnp.maximum(m0, m_b)
            a0 = jnp.exp(m0 - m_new)
            a1 = jnp.exp(m_b - m_new)
            l_new = a0 * l_ref[pair, rows, :] + a1 * l_b
            a_new = a0 * a_ref[pair, rows, :] + a1 * pv
        if final:
            o_ref[out_rows, pair * LANES:(pair + 1) * LANES] = (a_new / l_new).astype(o_ref.dtype)
        else:
            m_ref[pair, rows, :] = m_new
            l_ref[pair, rows, :] = l_new
            a_ref[pair, rows, :] = a_new

    @pl.when(ph < PHASES_PER_BRANCH)
    def _():
        bias = jnp.where(c == 0, bias_own, bias_band)
        for cl in range(4):
            r = ph * 4 + cl
            for pair in range(N_SLABS):
                ls = slice(pair * LANES, (pair + 1) * LANES)
                kk = jnp.concatenate([k16p_ref[cl, :, ls], k16_ref[cl, :, ls]], axis=0)
                vv = jnp.concatenate([v16p_ref[cl, :, ls], v16_ref[cl, :, ls]], axis=0)
                fresh = _pair_scores(q16_ref[cl, :, ls], kk, vv, bias)
                merge((pair, pl.ds(r, BLK, stride=16)), fresh, first=True, final=False)

    @pl.when((ph >= PHASES_PER_BRANCH) & (ph < 2 * PHASES_PER_BRANCH))
    def _():
        r = ph - PHASES_PER_BRANCH
        for n in range(TQ // BLK):
            for pair in range(N_SLABS):
                ls = slice(pair * LANES, (pair + 1) * LANES)
                if n == 0:
                    kk = jnp.concatenate([k4p_ref[:, ls], k4_ref[0:BLK, ls]], axis=0)
                    vv = jnp.concatenate([v4p_ref[:, ls], v4_ref[0:BLK, ls]], axis=0)
                    bias = jnp.where(c == 0, bias_own, bias_band)
                else:
                    kk = k4_ref[(n - 1) * BLK:(n + 1) * BLK, ls]
                    vv = v4_ref[(n - 1) * BLK:(n + 1) * BLK, ls]
                    bias = bias_band
                fresh = _pair_scores(q4_ref[n * BLK:(n + 1) * BLK, ls], kk, vv, bias)
                merge((pair, pl.ds(4 * BLK * n + r, BLK, stride=4)), fresh, first=False, final=False)

    @pl.when(ph >= 2 * PHASES_PER_BRANCH)
    def _():
        i = ph - 2 * PHASES_PER_BRANCH
        for n in range(TQ // BLK):
            for pair in range(N_SLABS):
                ls = slice(pair * LANES, (pair + 1) * LANES)
                if n == 0:
                    kk = jnp.concatenate([k1p_ref[:, ls], k1_ref[0:BLK, ls]], axis=0)
                    vv = jnp.concatenate([v1p_ref[:, ls], v1_ref[0:BLK, ls]], axis=0)
                    bias = jnp.where((c == 0) & (i == 0), bias_own, bias_band)
                else:
                    kk = k1_ref[(n - 1) * BLK:(n + 1) * BLK, ls]
                    vv = v1_ref[(n - 1) * BLK:(n + 1) * BLK, ls]
                    bias = bias_band
                fresh = _pair_scores(q1_ref[n * BLK:(n + 1) * BLK, ls], kk, vv, bias)
                rows = pl.ds(pl.multiple_of(i * TQ + n * BLK, BLK), BLK)
                merge((pair, rows), fresh, first=False, final=True, out_rows=slice(n * BLK, (n + 1) * BLK))


def _p_attn(q1, k1, v1, q4, k4, v4, q16, k16, v16):
    B, S, _ = q1.shape
    P = PHASES_PER_BRANCH
    assert S % CHUNK == 0 and TQ % BLK == 0
    d16 = pl.BlockSpec((None, 4, BLK, ATTN_W), lambda b, c, p: (b, _clip(p, P - 1), c, 0))
    d16p = pl.BlockSpec((None, 4, BLK, ATTN_W), lambda b, c, p: (b, _clip(p, P - 1), jnp.maximum(c - 1, 0), 0))
    d4 = pl.BlockSpec((None, None, TQ, ATTN_W), lambda b, c, p: (b, _clip(p - P, P - 1), c, 0))
    d4p = pl.BlockSpec((None, None, BLK, ATTN_W),
                       lambda b, c, p: (b, _clip(p - P, P - 1), jnp.maximum(c * (TQ // BLK) - 1, 0), 0))
    d1 = pl.BlockSpec((None, TQ, ATTN_W), lambda b, c, p: (b, c * P + _clip(p - 2 * P, P - 1), 0))
    d1p = pl.BlockSpec((None, BLK, ATTN_W),
                       lambda b, c, p: (b, jnp.maximum((c * P + _clip(p - 2 * P, P - 1)) * (TQ // BLK) - 1, 0), 0))
    return pl.pallas_call(
        _p_attn_kernel,
        grid=(B, S // CHUNK, 3 * P),
        in_specs=[d16, d16, d16p, d16, d16p, d4, d4, d4p, d4, d4p, d1, d1, d1p, d1, d1p],
        out_specs=d1,
        out_shape=jax.ShapeDtypeStruct((B, S, ATTN_W), BF16),
        scratch_shapes=[pltpu.VMEM((N_SLABS, CHUNK, LANES), F32)] * 3,
        compiler_params=_params(("arbitrary",) * 3),
        name="p_attn",
    )(q16, k16, k16, v16, v16, q4, k4, k4, v4, v4, q1, k1, k1, v1, v1)


def _p_memkv_kernel(mem_ref, g_ref, wk_ref, wv_ref, k_ref, v_ref):
    h = _rms(mem_ref[...], g_ref[...]).astype(BF16)
    k = _dot(h, wk_ref[...])
    v = _dot(h, wv_ref[...])
    for hd in range(N_XHEADS):
        k_ref[:, hd, :] = k[:, hd * XHEAD_DIM:(hd + 1) * XHEAD_DIM]
        v_ref[:, hd, :] = v[:, hd * XHEAD_DIM:(hd + 1) * XHEAD_DIM]


def _p_memkv(mem, g, wk, wv):
    B, M, _ = mem.shape
    blk = pl.BlockSpec((None, M, D_MODEL), lambda b: (b, 0, 0))
    oblk = pl.BlockSpec((None, M, N_XHEADS, XHEAD_DIM), lambda b: (b, 0, 0, 0))
    return pl.pallas_call(
        _p_memkv_kernel,
        grid=(B,),
        in_specs=[blk, _const_spec(g.shape), _const_spec(wk.shape), _const_spec(wv.shape)],
        out_specs=[oblk, oblk],
        out_shape=[jax.ShapeDtypeStruct((B, M, N_XHEADS, XHEAD_DIM), F32)] * 2,
        compiler_params=_params(("arbitrary",)),
        name="p_memkv",
    )(mem, g, wk, wv)


def _p_post_kernel(attn_ref, mc_ref, x_ref, mk_ref, mv_ref, goa_ref, wout_ref, gpm_ref,
                   gpre_ref, wq_ref, wo_ref, gpost_ref, o_ref):
    ma = _rms(attn_ref[...].astype(F32), goa_ref[...]).astype(BF16)
    y = _dot(ma, wout_ref[0:ATTN_W, :]) + _dot(mc_ref[...], wout_ref[ATTN_W:D_MODEL, :])
    x1 = x_ref[...] + _rms(y, gpm_ref[...])
    h = _rms(x1, gpre_ref[...]).astype(BF16)
    qx = (_dot(h, wq_ref[...]) * (XHEAD_DIM ** -0.5)).astype(BF16)
    ox = _xattn_heads(qx, mk_ref, mv_ref)
    o_ref[...] = x1 + _rms(_dot(ox, wo_ref[...]), gpost_ref[...])


def _p_post(attn, mc, x, mk, mv, goa, wout, gpm, gpre, wq, wo, gpost, tm):
    B, S, _ = x.shape
    tok = lambda c: pl.BlockSpec((None, tm, c), lambda b, j: (b, j, 0))
    mem = pl.BlockSpec((None, N_MEM, N_XHEADS, XHEAD_DIM), lambda b, j: (b, 0, 0, 0))
    consts = [goa, wout, gpm, gpre, wq, wo, gpost]
    return pl.pallas_call(
        _p_post_kernel,
        grid=(B, S // tm),
        in_specs=[tok(ATTN_W), tok(CONV_CH), tok(D_MODEL), mem, mem] + [_const_spec(a.shape) for a in consts],
        out_specs=tok(D_MODEL),
        out_shape=jax.ShapeDtypeStruct((B, S, D_MODEL), F32),
        compiler_params=_params(("arbitrary", "arbitrary")),
        name="p_post",
    )(attn, mc, x, mk, mv, *consts)


FFN_PAD = 8


def _p_ffn_kernel(x_ref, gpre_ref, wup_ref, cw_ref, wdn_ref, gpost_ref, y_ref, st_ref, ext_ref, g_ref, *, tm):
    j = pl.program_id(1)

    @pl.when(j == 0)
    def _():
        ext_ref[0:FFN_PAD, :] = jnp.zeros((FFN_PAD, D_FF), F32)

    x = x_ref[...]
    h = _rms(x, gpre_ref[...]).astype(BF16)
    for c0 in range(0, D_FF, FF_CHUNK):
        cs = slice(c0, c0 + FF_CHUNK)
        ext_ref[FFN_PAD:FFN_PAD + tm, cs] = _dot(h, wup_ref[:, cs])
        lin = _dot(h, wup_ref[:, D_FF + c0:D_FF + c0 + FF_CHUNK])
        f = (cw_ref[0:1, cs] * ext_ref[FFN_PAD - 2:FFN_PAD - 2 + tm, cs]
             + cw_ref[1:2, cs] * ext_ref[FFN_PAD - 1:FFN_PAD - 1 + tm, cs]
             + cw_ref[2:3, cs] * ext_ref[FFN_PAD:FFN_PAD + tm, cs])
        g_ref[:, cs] = _ffn_act(f, lin)
    tail = ext_ref[tm:tm + FFN_PAD, :]
    ext_ref[0:FFN_PAD, :] = tail
    st_ref[...] = tail[FFN_PAD - (FFN_K - 1):FFN_PAD, :]
    y_ref[...] = x + _rms(_dot(g_ref[...], wdn_ref[...]), gpost_ref[...])


def _p_ffn(x, gpre, wup, cw, wdn, gpost, tm):
    B, S, _ = x.shape
    tok = pl.BlockSpec((None, tm, D_MODEL), lambda b, j: (b, j, 0))
    st = pl.BlockSpec((None, FFN_K - 1, D_FF), lambda b, j: (b, 0, 0))
    return pl.pallas_call(
        functools.partial(_p_ffn_kernel, tm=tm),
        grid=(B, S // tm),
        in_specs=[tok, _const_spec(gpre.shape), _const_spec(wup.shape), _const_spec(cw.shape),
                  _const_spec(wdn.shape), _const_spec(gpost.shape)],
        out_specs=[tok, st],
        out_shape=[jax.ShapeDtypeStruct((B, S, D_MODEL), F32), jax.ShapeDtypeStruct((B, FFN_K - 1, D_FF), F32)],
        scratch_shapes=[pltpu.VMEM((FFN_PAD + tm, D_FF), F32), pltpu.VMEM((tm, D_FF), BF16)],
        compiler_params=_params(("arbitrary", "arbitrary")),
        name="p_ffn",
    )(x, gpre, wup, cw, wdn, gpost)


def _tcol(t, c):
    return slice(t * c, (t + 1) * c)


def _single_step_call(kernel, ins, out_shapes, name, scratch=()):
    return pl.pallas_call(
        kernel,
        grid=(1,),
        in_specs=[_const_spec(a.shape) for a in ins],
        out_specs=[pl.BlockSpec(s.shape, lambda *_, nd=len(s.shape): (0,) * nd) for s in out_shapes],
        out_shape=out_shapes,
        scratch_shapes=list(scratch),
        compiler_params=_params(("arbitrary",)),
        name=name,
    )(*ins)


def _s_inproj_kernel(x_ref, g_ref, w_ref, q_ref, k_ref, v_ref, kt_ref, vt_ref, u_ref):
    bd = x_ref.shape[0]
    g = g_ref[...]
    h = jnp.concatenate([_rms(x_ref[:, t, :], g).astype(BF16) for t in range(T_NEW)], axis=0)
    q = _dot(h, w_ref[:, 0:ATTN_W]) * (HEAD_DIM ** -0.5)
    k = _dot(h, w_ref[:, ATTN_W:2 * ATTN_W])
    v = _dot(h, w_ref[:, 2 * ATTN_W:3 * ATTN_W])
    a = _dot(h, w_ref[:, 3 * ATTN_W:3 * ATTN_W + CONV_CH])
    gate = _dot(h, w_ref[:, 3 * ATTN_W + CONV_CH:3 * ATTN_W + 2 * CONV_CH])
    u = a * _sigmoid(gate)
    for t in range(T_NEW):
        rows = slice(t * bd, (t + 1) * bd)
        q_ref[:, _tcol(t, ATTN_W)] = q[rows]
        k_ref[:, _tcol(t, ATTN_W)] = k[rows]
        v_ref[:, _tcol(t, ATTN_W)] = v[rows]
        kt_ref[t] = k[rows].T
        vt_ref[t] = v[rows].T
        u_ref[:, _tcol(t, CONV_CH)] = u[rows]


def _s_inproj(x, g, w_in):
    bd = x.shape[0]
    flat = jax.ShapeDtypeStruct((bd, T_NEW * ATTN_W), F32)
    tr = jax.ShapeDtypeStruct((T_NEW, ATTN_W, bd), F32)
    return _single_step_call(_s_inproj_kernel, [x, g, w_in], [flat, flat, flat, tr, tr, flat], "s_inproj")


SB = 8
NEW_PAD = 8


def _sample_key_multiplicity():
    t = np.arange(T_NEW)[:, None]
    dist = MAX_WINDOW + t - np.arange(MAX_WINDOW)[None, :]
    mult = np.zeros((T_NEW, MAX_WINDOW), np.float32)
    for d in DILATIONS:
        mult += (dist % d == 0) & (dist // d >= 1) & (dist // d <= BLK)
    tn = np.arange(NEW_PAD)[None, :]
    new = np.where(tn == t, float(len(DILATIONS)), np.where(tn < t, 1.0, 0.0)).astype(np.float32)
    return np.repeat(mult, N_HEADS, axis=0), np.repeat(new, N_HEADS, axis=0)


def _s_attn_kernel(q_ref, kn_ref, vn_ref, kt_ref, vt_ref, mw_ref, mn_ref, o_ref):
    ii = pl.program_id(1)
    rows8 = lax.broadcasted_iota(jnp.int32, (SB, ATTN_W), 0)
    sel = rows8 == ii

    @pl.when(ii == 0)
    def _():
        o_ref[...] = jnp.zeros(o_ref.shape, F32)

    def pick(ref, t):
        return jnp.sum(jnp.where(sel, ref[:, _tcol(t, ATTN_W)], 0.0), axis=0, keepdims=True)

    head_of_lane = lax.broadcasted_iota(jnp.int32, (N_HEADS, ATTN_W), 1) // HEAD_DIM
    diag = head_of_lane == lax.broadcasted_iota(jnp.int32, (N_HEADS, ATTN_W), 0)
    qbd = jnp.concatenate([jnp.where(diag, pick(q_ref, t), 0.0) for t in range(T_NEW)], axis=0).astype(BF16)
    pad = jnp.zeros((NEW_PAD - T_NEW, ATTN_W), F32)
    k_new = jnp.concatenate([pick(kn_ref, t) for t in range(T_NEW)] + [pad], axis=0).astype(BF16)
    v_new = jnp.concatenate([pick(vn_ref, t) for t in range(T_NEW)] + [pad], axis=0).astype(BF16)

    kt = kt_ref[...].reshape(ATTN_W, MAX_WINDOW).astype(BF16)
    vt = vt_ref[...].reshape(ATTN_W, MAX_WINDOW).astype(BF16)
    mw = mw_ref[...]
    mn = mn_ref[...]
    s_w = jnp.where(mw > 0.0, _dot(qbd, kt), NEG)
    s_n = jnp.where(mn > 0.0, _dot_nt(qbd, k_new), NEG)
    m = jnp.maximum(jnp.max(s_w, axis=-1, keepdims=True), jnp.max(s_n, axis=-1, keepdims=True))
    p_w = jnp.exp(s_w - m) * mw
    p_n = jnp.exp(s_n - m) * mn
    l = jnp.sum(p_w, axis=-1, keepdims=True) + jnp.sum(p_n, axis=-1, keepdims=True)
    o_full = _dot_nt((p_w / l).astype(BF16), vt) + _dot((p_n / l).astype(BF16), v_new)
    for t in range(T_NEW):
        o = jnp.sum(jnp.where(diag, o_full[t * N_HEADS:(t + 1) * N_HEADS], 0.0), axis=0, keepdims=True)
        cur = o_ref[:, _tcol(t, ATTN_W)]
        o_ref[:, _tcol(t, ATTN_W)] = jnp.where(sel, o, cur)


def _s_attn(q, kn, vn, win_kt, win_vt):
    bd = q.shape[0]
    mw, mn = _sample_key_multiplicity()
    rowblk = pl.BlockSpec((SB, T_NEW * ATTN_W), lambda g, i: (g, 0))
    win = pl.BlockSpec((None, N_HEADS, HEAD_DIM, MAX_WINDOW), lambda g, i: (g * SB + i, 0, 0, 0))
    return pl.pallas_call(
        _s_attn_kernel,
        grid=(bd // SB, SB),
        in_specs=[rowblk, rowblk, rowblk, win, win, _const_spec(mw.shape), _const_spec(mn.shape)],
        out_specs=rowblk,
        out_shape=jax.ShapeDtypeStruct((bd, T_NEW * ATTN_W), F32),
        compiler_params=_params(("arbitrary", "arbitrary")),
        name="s_attn",
    )(q, kn, vn, win_kt, win_vt, jnp.asarray(mw), jnp.asarray(mn))


def _s_mix_kernel(attn_ref, u_ref, st_ref, x_ref, cw_ref, cb_ref, lng_ref, lnb_ref, goa_ref, goc_ref,
                  wout_ref, gpm_ref, gpre_ref, wq_ref, x1_ref, qx_ref, nst_ref):
    bd = x_ref.shape[0]
    nctx = CONV_K - 1

    def ctx(r):
        return st_ref[r] if r < nctx else u_ref[:, _tcol(r - nctx, CONV_CH)]

    mcs, mas = [], []
    for t in range(T_NEW):
        acc = jnp.zeros((bd, CONV_CH), F32)
        for j in range(CONV_K):
            acc = acc + cw_ref[j:j + 1, :] * ctx(t + j)
        c = _conv_post(acc + cb_ref[...], lng_ref[...], lnb_ref[...])
        mcs.append(_rms(c, goc_ref[...]).astype(BF16))
        mas.append(_rms(attn_ref[:, _tcol(t, ATTN_W)], goa_ref[...]).astype(BF16))
    y = (_dot(jnp.concatenate(mas, axis=0), wout_ref[0:ATTN_W, :])
         + _dot(jnp.concatenate(mcs, axis=0), wout_ref[ATTN_W:D_MODEL, :]))
    hs = []
    for t in range(T_NEW):
        x1 = x_ref[:, t, :] + _rms(y[t * bd:(t + 1) * bd], gpm_ref[...])
        x1_ref[:, _tcol(t, D_MODEL)] = x1
        hs.append(_rms(x1, gpre_ref[...]).astype(BF16))
    q = _dot(jnp.concatenate(hs, axis=0), wq_ref[...]) * (XHEAD_DIM ** -0.5)
    for t in range(T_NEW):
        qx_ref[:, _tcol(t, D_MODEL)] = q[t * bd:(t + 1) * bd]
    for r in range(nctx):
        nst_ref[r] = ctx(r + T_NEW)


def _s_mix(attn, u, state, x, cw, cb, lng, lnb, goa, goc, wout, gpm, gpre, wq):
    bd = x.shape[0]
    full = jax.ShapeDtypeStruct((bd, T_NEW * D_MODEL), F32)
    outs = [full, full, jax.ShapeDtypeStruct(state.shape, F32)]
    ins = [attn, u, state, x, cw, cb, lng, lnb, goa, goc, wout, gpm, gpre, wq]
    return _single_step_call(_s_mix_kernel, ins, outs, "s_mix")


def _s_xcore_kernel(q_ref, mk_ref, mv_ref, o_ref):
    ii = pl.program_id(1)
    rows8 = lax.broadcasted_iota(jnp.int32, (SB, D_MODEL), 0)
    sel = rows8 == ii

    @pl.when(ii == 0)
    def _():
        o_ref[...] = jnp.zeros(o_ref.shape, F32)

    qx = jnp.zeros((SB, D_MODEL), F32)
    for t in range(T_NEW):
        qt = jnp.sum(jnp.where(sel, q_ref[:, _tcol(t, D_MODEL)], 0.0), axis=0, keepdims=True)
        qx = jnp.where(rows8 == t, qt, qx)
    ox = _xattn_heads(qx.astype(BF16), mk_ref, mv_ref).astype(F32)
    for t in range(T_NEW):
        cur = o_ref[:, _tcol(t, D_MODEL)]
        o_ref[:, _tcol(t, D_MODEL)] = jnp.where(sel, ox[t:t + 1, :], cur)


def _s_xcore(q, mem_k, mem_v):
    bd = q.shape[0]
    rowblk = pl.BlockSpec((SB, T_NEW * D_MODEL), lambda g, i: (g, 0))
    mem = pl.BlockSpec((None, N_MEM, N_XHEADS, XHEAD_DIM), lambda g, i: (g * SB + i, 0, 0, 0))
    return pl.pallas_call(
        _s_xcore_kernel,
        grid=(bd // SB, SB),
        in_specs=[rowblk, mem, mem],
        out_specs=rowblk,
        out_shape=jax.ShapeDtypeStruct((bd, T_NEW * D_MODEL), F32),
        compiler_params=_params(("arbitrary", "arbitrary")),
        name="s_xcore",
    )(q, mem_k, mem_v)


def _s_ffn_kernel(ox_ref, x1_ref, st_ref, wo_ref, gqx_ref, gpre_ref, wup_ref, cw_ref, wdn_ref, gpost_ref,
                  y_ref, nst_ref, g_ref):
    bd = x1_ref.shape[0]
    ox = jnp.concatenate([ox_ref[:, _tcol(t, D_MODEL)].astype(BF16) for t in range(T_NEW)], axis=0)
    yx = _dot(ox, wo_ref[...])
    x2 = [x1_ref[:, _tcol(t, D_MODEL)] + _rms(yx[t * bd:(t + 1) * bd], gqx_ref[...]) for t in range(T_NEW)]
    h = jnp.concatenate([_rms(x2[t], gpre_ref[...]).astype(BF16) for t in range(T_NEW)], axis=0)
    for c0 in range(0, D_FF, FF_CHUNK):
        cs = slice(c0, c0 + FF_CHUNK)
        act = _dot(h, wup_ref[:, cs])
        lin = _dot(h, wup_ref[:, D_FF + c0:D_FF + c0 + FF_CHUNK])
        a = [st_ref[:, r, cs] for r in range(FFN_K - 1)]
        a += [act[t * bd:(t + 1) * bd] for t in range(T_NEW)]
        for t in range(T_NEW):
            f = cw_ref[0:1, cs] * a[t] + cw_ref[1:2, cs] * a[t + 1] + cw_ref[2:3, cs] * a[t + 2]
            g_ref[t * bd:(t + 1) * bd, cs] = _ffn_act(f, lin[t * bd:(t + 1) * bd])
        for r in range(FFN_K - 1):
            nst_ref[:, r, cs] = a[T_NEW + r]
    y = _dot(g_ref[...], wdn_ref[...])
    for t in range(T_NEW):
        y_ref[:, t, :] = x2[t] + _rms(y[t * bd:(t + 1) * bd], gpost_ref[...])


def _s_ffn(ox, x1, state, wo, gqx, gpre, wup, cw, wdn, gpost):
    bd = x1.shape[0]
    outs = [jax.ShapeDtypeStruct((bd, T_NEW, D_MODEL), F32), jax.ShapeDtypeStruct(state.shape, F32)]
    return _single_step_call(_s_ffn_kernel, [ox, x1, state, wo, gqx, gpre, wup, cw, wdn, gpost], outs, "s_ffn",
                             scratch=[pltpu.VMEM((T_NEW * bd, D_FF), BF16)])


def kernel(x_prompt, x_sample, mem_prompt, cache_win_k, cache_win_v, state_conv, state_ffn_conv, cache_mem_k, cache_mem_v, g_pre_mix, w_in, conv_w, conv_b, ln_conv_g, ln_conv_b, g_out_attn, g_out_conv, w_out, g_post_mix, g_pre_x, g_mem, w_xq, w_mk, w_mv, w_xo, g_post_x, g_pre_ffn, w_up, ffn_conv_w, w_down, g_post_ffn):
    depth = w_in.shape[0]
    assert depth == 1
    B, S, _ = x_prompt.shape
    Bd, T, _ = x_sample.shape
    assert T == T_NEW and cache_win_k.shape[2] == MAX_WINDOW and S % CHUNK == 0 and Bd % SB == 0

    gpm, gpx, gm, gqx, gpf, gqf = (_row(a[0]) for a in (g_pre_mix, g_pre_x, g_mem, g_post_x, g_pre_ffn, g_post_ffn))
    gpo, goa, goc = _row(g_post_mix[0]), _row(g_out_attn[0]), _row(g_out_conv[0])
    cb, lng, lnb = _row(conv_b[0]), _row(ln_conv_g[0]), _row(ln_conv_b[0])
    cw, fcw = conv_w[0], ffn_conv_w[0]
    win, wout, wxq, wmk, wmv, wxo, wup, wdn = (
        a[0].astype(BF16) for a in (w_in, w_out, w_xq, w_mk, w_mv, w_xo, w_up, w_down))
    W = min(MAX_WINDOW, S)

    (q1, k1, v1, q4, k4, v4, q16, k16, v16, kt_p, vt_p, mc, cst_p) = _p_inproj(
        x_prompt, gpm, win, cw, cb, lng, lnb, goc, tm=512)
    attn = _p_attn(q1, k1, v1, q4, k4, v4, q16, k16, v16)
    mk_p, mv_p = _p_memkv(mem_prompt, gm, wmk, wmv)
    x2 = _p_post(attn, mc, x_prompt, mk_p, mv_p, goa, wout, gpo, gpx, wxq, wxo, gqx, tm=512)
    y_p, ffn_p = _p_ffn(x2, gpf, wup, fcw, wdn, gqf, tm=256)
    conv_p = cst_p[:, HALO - (CONV_K - 1):, :]
    to_win = lambda t: jnp.transpose(t.reshape(1, B, N_HEADS, HEAD_DIM, W), (0, 1, 4, 2, 3))

    qs, ks, vs, kt_s, vt_s, us = _s_inproj(x_sample, gpm, win)
    win_kt = jnp.transpose(cache_win_k[0], (0, 2, 3, 1))
    win_vt = jnp.transpose(cache_win_v[0], (0, 2, 3, 1))
    attn_s = _s_attn(qs, ks, vs, win_kt, win_vt)
    st_conv = jnp.transpose(state_conv[0], (1, 0, 2))
    x1s, qxs, conv_s = _s_mix(attn_s, us, st_conv, x_sample, cw, cb, lng, lnb, goa, goc, wout, gpo, gpx, wxq)
    oxs = _s_xcore(qxs, cache_mem_k[0], cache_mem_v[0])
    y_s, ffn_s = _s_ffn(oxs, x1s, state_ffn_conv[0], wxo, gqx, gpf, wup, fcw, wdn, gqf)
    to_rows = lambda t: jnp.transpose(t.reshape(1, T, N_HEADS, HEAD_DIM, Bd), (0, 4, 1, 2, 3))

    return (y_p, y_s, to_win(kt_p), to_win(vt_p), conv_p[None], ffn_p[None], mk_p[None], mv_p[None],
            to_rows(kt_s), to_rows(vt_s), jnp.transpose(conv_s, (1, 0, 2))[None], ffn_s[None])
```

```python
import functools
import math

import jax
import jax.numpy as jnp
import numpy as np
from jax import lax
from jax.experimental import pallas as pl
from jax.experimental.pallas import tpu as pltpu

F32 = jnp.float32
BF16 = jnp.bfloat16

EPS = 1e-6
D_MODEL = 1024
N_HEADS = 8
HEAD_DIM = 64
ATTN_W = N_HEADS * HEAD_DIM
CONV_CH = D_MODEL - ATTN_W
CONV_K = 31
BLK = 128
DILATIONS = (16, 4, 1)
MAX_WINDOW = 2048
N_MEM = 256
N_XHEADS = 4
XHEAD_DIM = D_MODEL // N_XHEADS
D_FF = 2816
FFN_K = 3
T_NEW = 4
NEG = -1e30
LANES = 128
N_SLABS = ATTN_W // LANES
FF_CHUNK = 256
VMEM_LIMIT = 56 * 1024 * 1024

CHUNK = 16 * BLK
PHASES_PER_BRANCH = 4
TQ = CHUNK // PHASES_PER_BRANCH


def _rms(x, g):
    return x * lax.rsqrt(jnp.mean(x * x, axis=-1, keepdims=True) + EPS) * g


def _dot(a, b):
    return jnp.dot(a, b, preferred_element_type=F32)


def _dot_nt(a, b):
    return lax.dot_general(a, b, (((1,), (1,)), ((), ())), preferred_element_type=F32)


def _sigmoid(x):
    return 1.0 / (1.0 + jnp.exp(-x))


def _gelu_tanh(x):
    c = math.sqrt(2.0 / math.pi)
    return x * (0.5 * (1.0 + jnp.tanh(c * (x + 0.044715 * (x * x * x)))))


def _conv_post(c, lng, lnb):
    mu = jnp.mean(c, axis=-1, keepdims=True)
    d = c - mu
    var = jnp.mean(d * d, axis=-1, keepdims=True)
    y = d * lax.rsqrt(var + EPS) * lng + lnb
    return y * _sigmoid(y)


def _ffn_act(f, lin):
    return (_gelu_tanh(f) * lin).astype(BF16)


def _xattn_heads(qx, mk_ref, mv_ref):
    outs = []
    for h in range(N_XHEADS):
        hs = slice(h * XHEAD_DIM, (h + 1) * XHEAD_DIM)
        s = _dot_nt(qx[:, hs], mk_ref[:, hs])
        m = jnp.max(s, axis=-1, keepdims=True)
        p = jnp.exp(s - m)
        l = jnp.sum(p, axis=-1, keepdims=True)
        outs.append(_dot((p / l).astype(BF16), mv_ref[:, hs]).astype(BF16))
    return jnp.concatenate(outs, axis=-1)


def _const_spec(shape):
    nd = len(shape)
    return pl.BlockSpec(shape, lambda *_: (0,) * nd, pipeline_mode=pl.Buffered(1))


def _params(sem):
    return pltpu.CompilerParams(dimension_semantics=sem, vmem_limit_bytes=VMEM_LIMIT)


def _row(v):
    return v.reshape(1, -1).astype(F32)


def _clip(v, hi):
    return jnp.minimum(jnp.maximum(v, 0), hi)


HALO = 32
CONV_ROWS = 64


def _p_inproj_kernel(x_ref, g_ref, w_ref, cw_ref, cb_ref, lng_ref, lnb_ref, goc_ref,
                     q1_ref, k1_ref, v1_ref, q4_ref, k4_ref, v4_ref, q16_ref, k16_ref, v16_ref,
                     kt_ref, vt_ref, mc_ref, cst_ref, z_ref, z4_ref_all, ext_ref, c_ref, *, tm):
    j = pl.program_id(1)

    @pl.when(j == 0)
    def _():
        ext_ref[:, 0:HALO, :] = jnp.zeros((N_SLABS, HALO, LANES), F32)

    @pl.when(j > 0)
    def _():
        ext_ref[:, 0:HALO, :] = ext_ref[:, tm:tm + HALO, :]

    h = _rms(x_ref[...], g_ref[...]).astype(BF16)
    a = _dot(h, w_ref[:, 3 * ATTN_W:3 * ATTN_W + CONV_CH])
    gate = _dot(h, w_ref[:, 3 * ATTN_W + CONV_CH:3 * ATTN_W + 2 * CONV_CH])
    u = a * _sigmoid(gate)
    cst_ref[...] = u[tm - HALO:tm, :]
    for s in range(N_SLABS):
        ext_ref[s, HALO:HALO + tm, :] = u[:, s * LANES:(s + 1) * LANES]

    def project(which, scale, tok_ref, d4_ref, d16_ref, t_ref):
        z = _dot(h, w_ref[:, which * ATTN_W:(which + 1) * ATTN_W])
        if scale != 1.0:
            z = z * scale
        tok_ref[...] = z.astype(BF16)
        if t_ref is not None:
            t_ref[...] = z.T
        zs_ref, z4_ref = z_ref.at[which], z4_ref_all.at[which]
        for s in range(N_SLABS):
            ls = slice(s * LANES, (s + 1) * LANES)
            zs_ref[s] = z[:, ls]
            for r in range(4):
                z4 = zs_ref[s, pl.ds(r, tm // 4, stride=4), :]
                d4_ref[r, :, ls] = z4.astype(BF16)
                z4_ref[s, r * (tm // 4):(r + 1) * (tm // 4), :] = z4
            for r in range(16):
                z16 = z4_ref[s, pl.ds((r % 4) * (tm // 4) + r // 4, tm // 16, stride=4), :]
                d16_ref[r, :, ls] = z16.astype(BF16)

    project(0, HEAD_DIM ** -0.5, q1_ref, q4_ref, q16_ref, None)
    project(1, 1.0, k1_ref, k4_ref, k16_ref, kt_ref)
    project(2, 1.0, v1_ref, v4_ref, v16_ref, vt_ref)

    off = HALO - (CONV_K - 1)
    for s in range(N_SLABS):
        ls = slice(s * LANES, (s + 1) * LANES)
        for r0 in range(0, tm, CONV_ROWS):
            acc = jnp.zeros((CONV_ROWS, LANES), F32)
            for t in range(CONV_K):
                acc = acc + cw_ref[t:t + 1, ls] * ext_ref[s, r0 + off + t:r0 + off + t + CONV_ROWS, :]
            c_ref[r0:r0 + CONV_ROWS, ls] = acc
    c = _conv_post(c_ref[...] + cb_ref[...], lng_ref[...], lnb_ref[...])
    mc_ref[...] = _rms(c, goc_ref[...]).astype(BF16)


def _p_inproj(x, g, w_in, cw, cb, lng, lnb, goc, tm):
    B, S, _ = x.shape
    W = min(MAX_WINDOW, S)
    nt = S // tm
    first = (S - W) // tm
    tok = lambda c: pl.BlockSpec((None, tm, c), lambda b, j: (b, j, 0))
    cls = lambda d: pl.BlockSpec((None, d, tm // d, ATTN_W), lambda b, j: (b, 0, j, 0))
    win = pl.BlockSpec((None, ATTN_W, tm), lambda b, j: (b, 0, jnp.maximum(j - first, 0)))
    cst = pl.BlockSpec((None, HALO, CONV_CH), lambda b, j: (b, 0, 0))
    bf = lambda *shape: jax.ShapeDtypeStruct(shape, BF16)
    consts = [g, w_in, cw, cb, lng, lnb, goc]
    return pl.pallas_call(
        functools.partial(_p_inproj_kernel, tm=tm),
        grid=(B, nt),
        in_specs=[tok(D_MODEL)] + [_const_spec(a.shape) for a in consts],
        out_specs=[tok(ATTN_W)] * 3 + [cls(4)] * 3 + [cls(16)] * 3 + [win, win, tok(CONV_CH), cst],
        out_shape=[bf(B, S, ATTN_W)] * 3 + [bf(B, 4, S // 4, ATTN_W)] * 3 + [bf(B, 16, S // 16, ATTN_W)] * 3
        + [jax.ShapeDtypeStruct((B, ATTN_W, W), F32)] * 2
        + [bf(B, S, CONV_CH), jax.ShapeDtypeStruct((B, HALO, CONV_CH), F32)],
        scratch_shapes=[pltpu.VMEM((3, N_SLABS, tm, LANES), F32)] * 2
        + [pltpu.VMEM((N_SLABS, HALO + tm, LANES), F32), pltpu.VMEM((tm, CONV_CH), F32)],
        compiler_params=_params(("arbitrary", "arbitrary")),
        name="p_inproj",
    )(x, *consts)


SM_ROWS = 32


def _pair_scores(qp, kk, vv, bias_ref, s_ref, pb_ref, mb_ref, lb_ref):
    lo = lax.broadcasted_iota(jnp.int32, (BLK, LANES), 1) < HEAD_DIM
    lo2 = lax.broadcasted_iota(jnp.int32, (2 * BLK, LANES), 1) < HEAD_DIM
    los = lax.broadcasted_iota(jnp.int32, (SM_ROWS, LANES), 1) < HEAD_DIM
    zq = jnp.zeros_like(qp)
    zv = jnp.zeros_like(vv)
    q2 = jnp.concatenate([jnp.where(lo, qp, zq), jnp.where(lo, zq, qp)], axis=0)
    s_ref[...] = _dot_nt(q2, kk)
    for r0 in range(0, BLK, SM_ROWS):
        rows = slice(r0, r0 + SM_ROWS)
        bias = bias_ref[rows, :]
        stats = []
        for hh in range(2):
            sc = s_ref[hh * BLK + r0:hh * BLK + r0 + SM_ROWS, :] + bias
            m = jnp.max(sc, axis=-1, keepdims=True)
            p = jnp.exp(sc - m)
            stats.append((m, jnp.sum(p, axis=-1, keepdims=True)))
            pb_ref[rows, hh * 2 * BLK:(hh + 1) * 2 * BLK] = p.astype(BF16)
        mb_ref[rows, :] = jnp.where(los, stats[0][0], stats[1][0])
        lb_ref[rows, :] = jnp.where(los, stats[0][1], stats[1][1])
    vcat = jnp.concatenate([jnp.where(lo2, vv, zv), jnp.where(lo2, zv, vv)], axis=0)
    return _dot(pb_ref[...], vcat)


def _p_attn_kernel(q16_ref, k16_ref, k16p_ref, v16_ref, v16p_ref,
                   q4_ref, k4_ref, k4p_ref, v4_ref, v4p_ref,
                   q1_ref, k1_ref, k1p_ref, v1_ref, v1p_ref,
                   o_ref, m_ref, l_ref, a_ref, bias_ref, s_ref, pb_ref, mb_ref, lb_ref):
    c = pl.program_id(1)
    ph = pl.program_id(2)

    @pl.when((c == 0) & (ph == 0))
    def _():
        row = lax.broadcasted_iota(jnp.int32, (BLK, 2 * BLK), 0)
        col = lax.broadcasted_iota(jnp.int32, (BLK, 2 * BLK), 1)
        band = (col >= row) & (col <= row + BLK)
        bias_ref[0] = jnp.where(band, 0.0, NEG)
        bias_ref[1] = jnp.where(band & (col >= BLK), 0.0, NEG)

    def unit(qp, kk, vv, seq_start, state_rows, first, final, out_rows=None):
        pair, rows = state_rows
        bias = bias_ref.at[jnp.where(seq_start, 1, 0)] if not isinstance(seq_start, bool) else bias_ref.at[int(seq_start)]
        buf = pair % 2
        pv = _pair_scores(qp, kk, vv, bias, s_ref.at[buf], pb_ref.at[buf], mb_ref.at[buf], lb_ref.at[buf])
        m_b = mb_ref[buf]
        l_b = lb_ref[buf]
        if first:
            m_new, l_new, a_new = m_b, l_b, pv
        else:
            m0 = m_ref[pair, rows, :]
            m_new = jnp.maximum(m0, m_b)
            a0 = jnp.exp(m0 - m_new)
            a1 = jnp.exp(m_b - m_new)
            l_new = a0 * l_ref[pair, rows, :] + a1 * l_b
            a_new = a0 * a_ref[pair, rows, :] + a1 * pv
        if final:
            o_ref[out_rows, pair * LANES:(pair + 1) * LANES] = (a_new / l_new).astype(o_ref.dtype)
        else:
            m_ref[pair, rows, :] = m_new
            l_ref[pair, rows, :] = l_new
            a_ref[pair, rows, :] = a_new

    @pl.when(ph < PHASES_PER_BRANCH)
    def _():
        for cl in range(4):
            r = ph * 4 + cl
            for pair in range(N_SLABS):
                ls = slice(pair * LANES, (pair + 1) * LANES)
                kk = jnp.concatenate([k16p_ref[cl, :, ls], k16_ref[cl, :, ls]], axis=0)
                vv = jnp.concatenate([v16p_ref[cl, :, ls], v16_ref[cl, :, ls]], axis=0)
                unit(q16_ref[cl, :, ls], kk, vv, c == 0, (pair, pl.ds(r, BLK, stride=16)), first=True, final=False)

    def contiguous_branch(q_ref, k_ref, kp_ref, v_ref, vp_ref, seq_start, rows_of, final):
        for n in range(TQ // BLK):
            for pair in range(N_SLABS):
                ls = slice(pair * LANES, (pair + 1) * LANES)
                if n == 0:
                    kk = jnp.concatenate([kp_ref[:, ls], k_ref[0:BLK, ls]], axis=0)
                    vv = jnp.concatenate([vp_ref[:, ls], v_ref[0:BLK, ls]], axis=0)
                else:
                    kk = k_ref[(n - 1) * BLK:(n + 1) * BLK, ls]
                    vv = v_ref[(n - 1) * BLK:(n + 1) * BLK, ls]
                unit(q_ref[n * BLK:(n + 1) * BLK, ls], kk, vv, seq_start if n == 0 else False, (pair, rows_of(n)),
                     first=False, final=final, out_rows=slice(n * BLK, (n + 1) * BLK))

    @pl.when((ph >= PHASES_PER_BRANCH) & (ph < 2 * PHASES_PER_BRANCH))
    def _():
        r = ph - PHASES_PER_BRANCH
        contiguous_branch(q4_ref, k4_ref, k4p_ref, v4_ref, v4p_ref, c == 0,
                          lambda n: pl.ds(4 * BLK * n + r, BLK, stride=4), final=False)

    @pl.when(ph >= 2 * PHASES_PER_BRANCH)
    def _():
        i = ph - 2 * PHASES_PER_BRANCH
        contiguous_branch(q1_ref, k1_ref, k1p_ref, v1_ref, v1p_ref, (c == 0) & (i == 0),
                          lambda n: pl.ds(pl.multiple_of(i * TQ + n * BLK, BLK), BLK), final=True)


def _p_attn(q1, k1, v1, q4, k4, v4, q16, k16, v16):
    B, S, _ = q1.shape
    P = PHASES_PER_BRANCH
    assert S % CHUNK == 0 and TQ % BLK == 0
    d16 = pl.BlockSpec((None, 4, BLK, ATTN_W), lambda b, c, p: (b, _clip(p, P - 1), c, 0))
    d16p = pl.BlockSpec((None, 4, BLK, ATTN_W), lambda b, c, p: (b, _clip(p, P - 1), jnp.maximum(c - 1, 0), 0))
    d4 = pl.BlockSpec((None, None, TQ, ATTN_W), lambda b, c, p: (b, _clip(p - P, P - 1), c, 0))
    d4p = pl.BlockSpec((None, None, BLK, ATTN_W),
                       lambda b, c, p: (b, _clip(p - P, P - 1), jnp.maximum(c * (TQ // BLK) - 1, 0), 0))
    d1 = pl.BlockSpec((None, TQ, ATTN_W), lambda b, c, p: (b, c * P + _clip(p - 2 * P, P - 1), 0))
    d1p = pl.BlockSpec((None, BLK, ATTN_W),
                       lambda b, c, p: (b, jnp.maximum((c * P + _clip(p - 2 * P, P - 1)) * (TQ // BLK) - 1, 0), 0))
    return pl.pallas_call(
        _p_attn_kernel,
        grid=(B, S // CHUNK, 3 * P),
        in_specs=[d16, d16, d16p, d16, d16p, d4, d4, d4p, d4, d4p, d1, d1, d1p, d1, d1p],
        out_specs=d1,
        out_shape=jax.ShapeDtypeStruct((B, S, ATTN_W), BF16),
        scratch_shapes=[pltpu.VMEM((N_SLABS, CHUNK, LANES), F32)] * 3
        + [pltpu.VMEM((2, BLK, 2 * BLK), F32), pltpu.VMEM((2, 2 * BLK, 2 * BLK), F32),
           pltpu.VMEM((2, BLK, 4 * BLK), BF16), pltpu.VMEM((2, BLK, LANES), F32), pltpu.VMEM((2, BLK, LANES), F32)],
        compiler_params=_params(("arbitrary",) * 3),
        name="p_attn",
    )(q16, k16, k16, v16, v16, q4, k4, k4, v4, v4, q1, k1, k1, v1, v1)


def _p_memkv_kernel(mem_ref, g_ref, wk_ref, wv_ref, k_ref, v_ref, kb_ref, vb_ref):
    h = _rms(mem_ref[...], g_ref[...]).astype(BF16)
    k = _dot(h, wk_ref[...])
    v = _dot(h, wv_ref[...])
    kb_ref[...] = k.astype(BF16)
    vb_ref[...] = v.astype(BF16)
    for hd in range(N_XHEADS):
        k_ref[:, hd, :] = k[:, hd * XHEAD_DIM:(hd + 1) * XHEAD_DIM]
        v_ref[:, hd, :] = v[:, hd * XHEAD_DIM:(hd + 1) * XHEAD_DIM]


def _p_memkv(mem, g, wk, wv):
    B, M, _ = mem.shape
    blk = pl.BlockSpec((None, M, D_MODEL), lambda b: (b, 0, 0))
    oblk = pl.BlockSpec((None, M, N_XHEADS, XHEAD_DIM), lambda b: (b, 0, 0, 0))
    return pl.pallas_call(
        _p_memkv_kernel,
        grid=(B,),
        in_specs=[blk, _const_spec(g.shape), _const_spec(wk.shape), _const_spec(wv.shape)],
        out_specs=[oblk, oblk, blk, blk],
        out_shape=[jax.ShapeDtypeStruct((B, M, N_XHEADS, XHEAD_DIM), F32)] * 2
        + [jax.ShapeDtypeStruct((B, M, D_MODEL), BF16)] * 2,
        compiler_params=_params(("arbitrary",)),
        name="p_memkv",
    )(mem, g, wk, wv)


def _p_post_kernel(attn_ref, mc_ref, x_ref, mk_ref, mv_ref, goa_ref, wout_ref, gpm_ref,
                   gpre_ref, wq_ref, wo_ref, gpost_ref, o_ref):
    ma = _rms(attn_ref[...].astype(F32), goa_ref[...]).astype(BF16)
    y = _dot(ma, wout_ref[0:ATTN_W, :]) + _dot(mc_ref[...], wout_ref[ATTN_W:D_MODEL, :])
    x1 = x_ref[...] + _rms(y, gpm_ref[...])
    h = _rms(x1, gpre_ref[...]).astype(BF16)
    qx = (_dot(h, wq_ref[...]) * (XHEAD_DIM ** -0.5)).astype(BF16)
    ox = _xattn_heads(qx, mk_ref, mv_ref)
    o_ref[...] = x1 + _rms(_dot(ox, wo_ref[...]), gpost_ref[...])


def _p_post(attn, mc, x, mk, mv, goa, wout, gpm, gpre, wq, wo, gpost, tm):
    B, S, _ = x.shape
    tok = lambda c: pl.BlockSpec((None, tm, c), lambda b, j: (b, j, 0))
    mem = pl.BlockSpec((None, N_MEM, D_MODEL), lambda b, j: (b, 0, 0))
    consts = [goa, wout, gpm, gpre, wq, wo, gpost]
    return pl.pallas_call(
        _p_post_kernel,
        grid=(B, S // tm),
        in_specs=[tok(ATTN_W), tok(CONV_CH), tok(D_MODEL), mem, mem] + [_const_spec(a.shape) for a in consts],
        out_specs=tok(D_MODEL),
        out_shape=jax.ShapeDtypeStruct((B, S, D_MODEL), F32),
        compiler_params=_params(("arbitrary", "arbitrary")),
        name="p_post",
    )(attn, mc, x, mk, mv, *consts)


FFN_PAD = 8


def _p_ffn_kernel(x_ref, gpre_ref, wup_ref, cw_ref, wdn_ref, gpost_ref, y_ref, st_ref, ext_ref, g_ref, *, tm):
    j = pl.program_id(1)

    @pl.when(j == 0)
    def _():
        ext_ref[0:FFN_PAD, :] = jnp.zeros((FFN_PAD, D_FF), F32)

    x = x_ref[...]
    h = _rms(x, gpre_ref[...]).astype(BF16)
    for c0 in range(0, D_FF, FF_CHUNK):
        cs = slice(c0, c0 + FF_CHUNK)
        ext_ref[FFN_PAD:FFN_PAD + tm, cs] = _dot(h, wup_ref[:, cs])
        lin = _dot(h, wup_ref[:, D_FF + c0:D_FF + c0 + FF_CHUNK])
        f = (cw_ref[0:1, cs] * ext_ref[FFN_PAD - 2:FFN_PAD - 2 + tm, cs]
             + cw_ref[1:2, cs] * ext_ref[FFN_PAD - 1:FFN_PAD - 1 + tm, cs]
             + cw_ref[2:3, cs] * ext_ref[FFN_PAD:FFN_PAD + tm, cs])
        g_ref[:, cs] = _ffn_act(f, lin)
    tail = ext_ref[tm:tm + FFN_PAD, :]
    ext_ref[0:FFN_PAD, :] = tail
    st_ref[...] = tail[FFN_PAD - (FFN_K - 1):FFN_PAD, :]
    y_ref[...] = x + _rms(_dot(g_ref[...], wdn_ref[...]), gpost_ref[...])


def _p_ffn(x, gpre, wup, cw, wdn, gpost, tm):
    B, S, _ = x.shape
    tok = pl.BlockSpec((None, tm, D_MODEL), lambda b, j: (b, j, 0))
    st = pl.BlockSpec((None, FFN_K - 1, D_FF), lambda b, j: (b, 0, 0))
    return pl.pallas_call(
        functools.partial(_p_ffn_kernel, tm=tm),
        grid=(B, S // tm),
        in_specs=[tok, _const_spec(gpre.shape), _const_spec(wup.shape), _const_spec(cw.shape),
                  _const_spec(wdn.shape), _const_spec(gpost.shape)],
        out_specs=[tok, st],
        out_shape=[jax.ShapeDtypeStruct((B, S, D_MODEL), F32), jax.ShapeDtypeStruct((B, FFN_K - 1, D_FF), F32)],
        scratch_shapes=[pltpu.VMEM((FFN_PAD + tm, D_FF), F32), pltpu.VMEM((tm, D_FF), BF16)],
        compiler_params=_params(("arbitrary", "arbitrary")),
        name="p_ffn",
    )(x, gpre, wup, cw, wdn, gpost)


def _tcol(t, c):
    return slice(t * c, (t + 1) * c)


def _single_step_call(kernel, ins, out_shapes, name, scratch=()):
    return pl.pallas_call(
        kernel,
        grid=(1,),
        in_specs=[_const_spec(a.shape) for a in ins],
        out_specs=[pl.BlockSpec(s.shape, lambda *_, nd=len(s.shape): (0,) * nd) for s in out_shapes],
        out_shape=out_shapes,
        scratch_shapes=list(scratch),
        compiler_params=_params(("arbitrary",)),
        name=name,
    )(*ins)


def _s_inproj_kernel(x_ref, g_ref, w_ref, q_ref, k_ref, v_ref, kt_ref, vt_ref, u_ref):
    bd = x_ref.shape[0]
    g = g_ref[...]
    h = jnp.concatenate([_rms(x_ref[:, t, :], g).astype(BF16) for t in range(T_NEW)], axis=0)
    q = _dot(h, w_ref[:, 0:ATTN_W]) * (HEAD_DIM ** -0.5)
    k = _dot(h, w_ref[:, ATTN_W:2 * ATTN_W])
    v = _dot(h, w_ref[:, 2 * ATTN_W:3 * ATTN_W])
    a = _dot(h, w_ref[:, 3 * ATTN_W:3 * ATTN_W + CONV_CH])
    gate = _dot(h, w_ref[:, 3 * ATTN_W + CONV_CH:3 * ATTN_W + 2 * CONV_CH])
    u = a * _sigmoid(gate)
    for t in range(T_NEW):
        rows = slice(t * bd, (t + 1) * bd)
        q_ref[:, _tcol(t, ATTN_W)] = q[rows]
        k_ref[:, _tcol(t, ATTN_W)] = k[rows]
        v_ref[:, _tcol(t, ATTN_W)] = v[rows]
        kt_ref[t] = k[rows].T
        vt_ref[t] = v[rows].T
        u_ref[:, _tcol(t, CONV_CH)] = u[rows]


def _s_inproj(x, g, w_in):
    bd = x.shape[0]
    flat = jax.ShapeDtypeStruct((bd, T_NEW * ATTN_W), F32)
    tr = jax.ShapeDtypeStruct((T_NEW, ATTN_W, bd), F32)
    return _single_step_call(_s_inproj_kernel, [x, g, w_in], [flat, flat, flat, tr, tr, flat], "s_inproj")


SB = 8
NEW_PAD = 8


def _sample_key_multiplicity():
    t = np.arange(T_NEW)[:, None]
    dist = MAX_WINDOW + t - np.arange(MAX_WINDOW)[None, :]
    mult = np.zeros((T_NEW, MAX_WINDOW), np.float32)
    for d in DILATIONS:
        mult += (dist % d == 0) & (dist // d >= 1) & (dist // d <= BLK)
    tn = np.arange(NEW_PAD)[None, :]
    new = np.where(tn == t, float(len(DILATIONS)), np.where(tn < t, 1.0, 0.0)).astype(np.float32)
    return np.repeat(mult, N_HEADS, axis=0), np.repeat(new, N_HEADS, axis=0)


def _s_attn_kernel(q_ref, kn_ref, vn_ref, kt_ref, vt_ref, mw_ref, mn_ref, o_ref):
    ii = pl.program_id(1)
    rows8 = lax.broadcasted_iota(jnp.int32, (SB, ATTN_W), 0)
    sel = rows8 == ii

    @pl.when(ii == 0)
    def _():
        o_ref[...] = jnp.zeros(o_ref.shape, F32)

    def pick(ref, t):
        return jnp.sum(jnp.where(sel, ref[:, _tcol(t, ATTN_W)], 0.0), axis=0, keepdims=True)

    head_of_lane = lax.broadcasted_iota(jnp.int32, (N_HEADS, ATTN_W), 1) // HEAD_DIM
    diag = head_of_lane == lax.broadcasted_iota(jnp.int32, (N_HEADS, ATTN_W), 0)
    qbd = jnp.concatenate([jnp.where(diag, pick(q_ref, t), 0.0) for t in range(T_NEW)], axis=0).astype(BF16)
    pad = jnp.zeros((NEW_PAD - T_NEW, ATTN_W), F32)
    k_new = jnp.concatenate([pick(kn_ref, t) for t in range(T_NEW)] + [pad], axis=0).astype(BF16)
    v_new = jnp.concatenate([pick(vn_ref, t) for t in range(T_NEW)] + [pad], axis=0).astype(BF16)

    kt = kt_ref[...].reshape(ATTN_W, MAX_WINDOW).astype(BF16)
    vt = vt_ref[...].reshape(ATTN_W, MAX_WINDOW).astype(BF16)
    mw = mw_ref[...]
    mn = mn_ref[...]
    s_w = jnp.where(mw > 0.0, _dot(qbd, kt), NEG)
    s_n = jnp.where(mn > 0.0, _dot_nt(qbd, k_new), NEG)
    m = jnp.maximum(jnp.max(s_w, axis=-1, keepdims=True), jnp.max(s_n, axis=-1, keepdims=True))
    p_w = jnp.exp(s_w - m) * mw
    p_n = jnp.exp(s_n - m) * mn
    l = jnp.sum(p_w, axis=-1, keepdims=True) + jnp.sum(p_n, axis=-1, keepdims=True)
    o_full = _dot_nt((p_w / l).astype(BF16), vt) + _dot((p_n / l).astype(BF16), v_new)
    for t in range(T_NEW):
        o = jnp.sum(jnp.where(diag, o_full[t * N_HEADS:(t + 1) * N_HEADS], 0.0), axis=0, keepdims=True)
        cur = o_ref[:, _tcol(t, ATTN_W)]
        o_ref[:, _tcol(t, ATTN_W)] = jnp.where(sel, o, cur)


def _s_attn(q, kn, vn, win_kt, win_vt):
    bd = q.shape[0]
    mw, mn = _sample_key_multiplicity()
    rowblk = pl.BlockSpec((SB, T_NEW * ATTN_W), lambda g, i: (g, 0))
    win = pl.BlockSpec((None, N_HEADS, HEAD_DIM, MAX_WINDOW), lambda g, i: (g * SB + i, 0, 0, 0))
    return pl.pallas_call(
        _s_attn_kernel,
        grid=(bd // SB, SB),
        in_specs=[rowblk, rowblk, rowblk, win, win, _const_spec(mw.shape), _const_spec(mn.shape)],
        out_specs=rowblk,
        out_shape=jax.ShapeDtypeStruct((bd, T_NEW * ATTN_W), F32),
        compiler_params=_params(("arbitrary", "arbitrary")),
        name="s_attn",
    )(q, kn, vn, win_kt, win_vt, jnp.asarray(mw), jnp.asarray(mn))


def _s_mix_kernel(attn_ref, u_ref, st_ref, x_ref, cw_ref, cb_ref, lng_ref, lnb_ref, goa_ref, goc_ref,
                  wout_ref, gpm_ref, gpre_ref, wq_ref, x1_ref, qx_ref, nst_ref):
    bd = x_ref.shape[0]
    nctx = CONV_K - 1

    def ctx(r):
        return st_ref[r] if r < nctx else u_ref[:, _tcol(r - nctx, CONV_CH)]

    mcs, mas = [], []
    for t in range(T_NEW):
        acc = jnp.zeros((bd, CONV_CH), F32)
        for j in range(CONV_K):
            acc = acc + cw_ref[j:j + 1, :] * ctx(t + j)
        c = _conv_post(acc + cb_ref[...], lng_ref[...], lnb_ref[...])
        mcs.append(_rms(c, goc_ref[...]).astype(BF16))
        mas.append(_rms(attn_ref[:, _tcol(t, ATTN_W)], goa_ref[...]).astype(BF16))
    y = (_dot(jnp.concatenate(mas, axis=0), wout_ref[0:ATTN_W, :])
         + _dot(jnp.concatenate(mcs, axis=0), wout_ref[ATTN_W:D_MODEL, :]))
    hs = []
    for t in range(T_NEW):
        x1 = x_ref[:, t, :] + _rms(y[t * bd:(t + 1) * bd], gpm_ref[...])
        x1_ref[:, _tcol(t, D_MODEL)] = x1
        hs.append(_rms(x1, gpre_ref[...]).astype(BF16))
    q = _dot(jnp.concatenate(hs, axis=0), wq_ref[...]) * (XHEAD_DIM ** -0.5)
    for t in range(T_NEW):
        qx_ref[:, _tcol(t, D_MODEL)] = q[t * bd:(t + 1) * bd]
    for r in range(nctx):
        nst_ref[r] = ctx(r + T_NEW)


def _s_mix(attn, u, state, x, cw, cb, lng, lnb, goa, goc, wout, gpm, gpre, wq):
    bd = x.shape[0]
    full = jax.ShapeDtypeStruct((bd, T_NEW * D_MODEL), F32)
    outs = [full, full, jax.ShapeDtypeStruct(state.shape, F32)]
    ins = [attn, u, state, x, cw, cb, lng, lnb, goa, goc, wout, gpm, gpre, wq]
    return _single_step_call(_s_mix_kernel, ins, outs, "s_mix")


def _s_xcore_kernel(q_ref, mk_ref, mv_ref, o_ref):
    ii = pl.program_id(1)
    rows8 = lax.broadcasted_iota(jnp.int32, (SB, D_MODEL), 0)
    sel = rows8 == ii

    @pl.when(ii == 0)
    def _():
        o_ref[...] = jnp.zeros(o_ref.shape, F32)

    nq = T_NEW * N_XHEADS
    qrow = lax.broadcasted_iota(jnp.int32, (nq, XHEAD_DIM), 0)
    q16 = jnp.zeros((nq, XHEAD_DIM), F32)
    for t in range(T_NEW):
        qt = jnp.sum(jnp.where(sel, q_ref[:, _tcol(t, D_MODEL)], 0.0), axis=0, keepdims=True)
        for h in range(N_XHEADS):
            q16 = jnp.where(qrow == t * N_XHEADS + h, qt[:, _tcol(h, XHEAD_DIM)], q16)
    k2 = mk_ref[...].reshape(N_MEM * N_XHEADS, XHEAD_DIM).astype(BF16)
    v2 = mv_ref[...].reshape(N_MEM * N_XHEADS, XHEAD_DIM).astype(BF16)
    srow = lax.broadcasted_iota(jnp.int32, (nq, N_MEM * N_XHEADS), 0)
    scol = lax.broadcasted_iota(jnp.int32, (nq, N_MEM * N_XHEADS), 1)
    s = jnp.where(srow % N_XHEADS == scol % N_XHEADS, _dot_nt(q16.astype(BF16), k2), NEG)
    m = jnp.max(s, axis=-1, keepdims=True)
    p = jnp.exp(s - m)
    l = jnp.sum(p, axis=-1, keepdims=True)
    o16 = _dot((p / l).astype(BF16), v2)
    for t in range(T_NEW):
        ot = jnp.concatenate([o16[t * N_XHEADS + h:t * N_XHEADS + h + 1, :] for h in range(N_XHEADS)], axis=1)
        cur = o_ref[:, _tcol(t, D_MODEL)]
        o_ref[:, _tcol(t, D_MODEL)] = jnp.where(sel, ot, cur)


def _s_xcore(q, mem_k, mem_v):
    bd = q.shape[0]
    rowblk = pl.BlockSpec((SB, T_NEW * D_MODEL), lambda g, i: (g, 0))
    mem = pl.BlockSpec((None, N_MEM, N_XHEADS, XHEAD_DIM), lambda g, i: (g * SB + i, 0, 0, 0))
    return pl.pallas_call(
        _s_xcore_kernel,
        grid=(bd // SB, SB),
        in_specs=[rowblk, mem, mem],
        out_specs=rowblk,
        out_shape=jax.ShapeDtypeStruct((bd, T_NEW * D_MODEL), F32),
        compiler_params=_params(("arbitrary", "arbitrary")),
        name="s_xcore",
    )(q, mem_k, mem_v)


def _s_ffn_kernel(ox_ref, x1_ref, st_ref, wo_ref, gqx_ref, gpre_ref, wup_ref, cw_ref, wdn_ref, gpost_ref,
                  y_ref, nst_ref, g_ref):
    bd = x1_ref.shape[0]
    ox = jnp.concatenate([ox_ref[:, _tcol(t, D_MODEL)].astype(BF16) for t in range(T_NEW)], axis=0)
    yx = _dot(ox, wo_ref[...])
    x2 = [x1_ref[:, _tcol(t, D_MODEL)] + _rms(yx[t * bd:(t + 1) * bd], gqx_ref[...]) for t in range(T_NEW)]
    h = jnp.concatenate([_rms(x2[t], gpre_ref[...]).astype(BF16) for t in range(T_NEW)], axis=0)
    for c0 in range(0, D_FF, FF_CHUNK):
        cs = slice(c0, c0 + FF_CHUNK)
        act = _dot(h, wup_ref[:, cs])
        lin = _dot(h, wup_ref[:, D_FF + c0:D_FF + c0 + FF_CHUNK])
        a = [st_ref[:, r, cs] for r in range(FFN_K - 1)]
        a += [act[t * bd:(t + 1) * bd] for t in range(T_NEW)]
        for t in range(T_NEW):
            f = cw_ref[0:1, cs] * a[t] + cw_ref[1:2, cs] * a[t + 1] + cw_ref[2:3, cs] * a[t + 2]
            g_ref[t * bd:(t + 1) * bd, cs] = _ffn_act(f, lin[t * bd:(t + 1) * bd])
        for r in range(FFN_K - 1):
            nst_ref[:, r, cs] = a[T_NEW + r]
    y = _dot(g_ref[...], wdn_ref[...])
    for t in range(T_NEW):
        y_ref[:, t, :] = x2[t] + _rms(y[t * bd:(t + 1) * bd], gpost_ref[...])


def _s_ffn(ox, x1, state, wo, gqx, gpre, wup, cw, wdn, gpost):
    bd = x1.shape[0]
    outs = [jax.ShapeDtypeStruct((bd, T_NEW, D_MODEL), F32), jax.ShapeDtypeStruct(state.shape, F32)]
    return _single_step_call(_s_ffn_kernel, [ox, x1, state, wo, gqx, gpre, wup, cw, wdn, gpost], outs, "s_ffn",
                             scratch=[pltpu.VMEM((T_NEW * bd, D_FF), BF16)])


def kernel(x_prompt, x_sample, mem_prompt, cache_win_k, cache_win_v, state_conv, state_ffn_conv, cache_mem_k, cache_mem_v, g_pre_mix, w_in, conv_w, conv_b, ln_conv_g, ln_conv_b, g_out_attn, g_out_conv, w_out, g_post_mix, g_pre_x, g_mem, w_xq, w_mk, w_mv, w_xo, g_post_x, g_pre_ffn, w_up, ffn_conv_w, w_down, g_post_ffn):
    depth = w_in.shape[0]
    assert depth == 1
    B, S, _ = x_prompt.shape
    Bd, T, _ = x_sample.shape
    assert T == T_NEW and cache_win_k.shape[2] == MAX_WINDOW and S % CHUNK == 0 and Bd % SB == 0

    gpm, gpx, gm, gqx, gpf, gqf = (_row(a[0]) for a in (g_pre_mix, g_pre_x, g_mem, g_post_x, g_pre_ffn, g_post_ffn))
    gpo, goa, goc = _row(g_post_mix[0]), _row(g_out_attn[0]), _row(g_out_conv[0])
    cb, lng, lnb = _row(conv_b[0]), _row(ln_conv_g[0]), _row(ln_conv_b[0])
    cw, fcw = conv_w[0], ffn_conv_w[0]
    win, wout, wxq, wmk, wmv, wxo, wup, wdn = (
        a[0].astype(BF16) for a in (w_in, w_out, w_xq, w_mk, w_mv, w_xo, w_up, w_down))
    W = min(MAX_WINDOW, S)

    (q1, k1, v1, q4, k4, v4, q16, k16, v16, kt_p, vt_p, mc, cst_p) = _p_inproj(
        x_prompt, gpm, win, cw, cb, lng, lnb, goc, tm=512)
    attn = _p_attn(q1, k1, v1, q4, k4, v4, q16, k16, v16)
    mk_p, mv_p, mkb, mvb = _p_memkv(mem_prompt, gm, wmk, wmv)
    x2 = _p_post(attn, mc, x_prompt, mkb, mvb, goa, wout, gpo, gpx, wxq, wxo, gqx, tm=512)
    y_p, ffn_p = _p_ffn(x2, gpf, wup, fcw, wdn, gqf, tm=256)
    conv_p = cst_p[:, HALO - (CONV_K - 1):, :]
    to_win = lambda t: jnp.transpose(t.reshape(1, B, N_HEADS, HEAD_DIM, W), (0, 1, 4, 2, 3))

    qs, ks, vs, kt_s, vt_s, us = _s_inproj(x_sample, gpm, win)
    win_kt = jnp.transpose(cache_win_k[0], (0, 2, 3, 1))
    win_vt = jnp.transpose(cache_win_v[0], (0, 2, 3, 1))
    attn_s = _s_attn(qs, ks, vs, win_kt, win_vt)
    st_conv = jnp.transpose(state_conv[0], (1, 0, 2))
    x1s, qxs, conv_s = _s_mix(attn_s, us, st_conv, x_sample, cw, cb, lng, lnb, goa, goc, wout, gpo, gpx, wxq)
    oxs = _s_xcore(qxs, cache_mem_k[0], cache_mem_v[0])
    y_s, ffn_s = _s_ffn(oxs, x1s, state_ffn_conv[0], wxo, gqx, gpf, wup, fcw, wdn, gqf)
    to_rows = lambda t: jnp.transpose(t.reshape(1, T, N_HEADS, HEAD_DIM, Bd), (0, 4, 1, 2, 3))

    return (y_p, y_s, to_win(kt_p), to_win(vt_p), conv_p[None], ffn_p[None], mk_p[None], mv_p[None],
            to_rows(kt_s), to_rows(vt_s), jnp.transpose(conv_s, (1, 0, 2))[None], ffn_s[None])
```

```python
import functools
import math

import jax
import jax.numpy as jnp
import numpy as np
from jax import lax
from jax.experimental import pallas as pl
from jax.experimental.pallas import tpu as pltpu

F32 = jnp.float32
BF16 = jnp.bfloat16

EPS = 1e-6
D_MODEL = 1024
N_HEADS = 8
HEAD_DIM = 64
ATTN_W = N_HEADS * HEAD_DIM
CONV_CH = D_MODEL - ATTN_W
CONV_K = 31
BLK = 128
DILATIONS = (16, 4, 1)
MAX_WINDOW = 2048
N_MEM = 256
N_XHEADS = 4
XHEAD_DIM = D_MODEL // N_XHEADS
D_FF = 2816
FFN_K = 3
T_NEW = 4
NEG = -1e30
LANES = 128
N_SLABS = ATTN_W // LANES
FF_CHUNK = 256
VMEM_LIMIT = 56 * 1024 * 1024

CHUNK = 16 * BLK
PHASES_PER_BRANCH = 4
TQ = CHUNK // PHASES_PER_BRANCH


def _rms(x, g):
    return x * lax.rsqrt(jnp.mean(x * x, axis=-1, keepdims=True) + EPS) * g


def _dot(a, b):
    return jnp.dot(a, b, preferred_element_type=F32)


def _dot_nt(a, b):
    return lax.dot_general(a, b, (((1,), (1,)), ((), ())), preferred_element_type=F32)


def _sigmoid(x):
    return 1.0 / (1.0 + jnp.exp(-x))


def _gelu_tanh(x):
    c = math.sqrt(2.0 / math.pi)
    return x * (0.5 * (1.0 + jnp.tanh(c * (x + 0.044715 * (x * x * x)))))


def _conv_post(c, lng, lnb):
    mu = jnp.mean(c, axis=-1, keepdims=True)
    d = c - mu
    var = jnp.mean(d * d, axis=-1, keepdims=True)
    y = d * lax.rsqrt(var + EPS) * lng + lnb
    return y * _sigmoid(y)


def _ffn_act(f, lin):
    return (_gelu_tanh(f) * lin).astype(BF16)


def _xattn_heads(qx, mk_ref, mv_ref):
    outs = []
    for h in range(N_XHEADS):
        hs = slice(h * XHEAD_DIM, (h + 1) * XHEAD_DIM)
        s = _dot_nt(qx[:, hs], mk_ref[:, hs])
        m = jnp.max(s, axis=-1, keepdims=True)
        p = jnp.exp(s - m)
        l = jnp.sum(p, axis=-1, keepdims=True)
        outs.append(_dot((p / l).astype(BF16), mv_ref[:, hs]).astype(BF16))
    return jnp.concatenate(outs, axis=-1)


def _const_spec(shape):
    nd = len(shape)
    return pl.BlockSpec(shape, lambda *_: (0,) * nd, pipeline_mode=pl.Buffered(1))


def _params(sem):
    return pltpu.CompilerParams(dimension_semantics=sem, vmem_limit_bytes=VMEM_LIMIT)


def _row(v):
    return v.reshape(1, -1).astype(F32)


def _clip(v, hi):
    return jnp.minimum(jnp.maximum(v, 0), hi)


HALO = 32
CONV_ROWS = 64


def _p_inproj_kernel(x_ref, g_ref, w_ref,
                     q1_ref, k1_ref, v1_ref, q4_ref, k4_ref, v4_ref, q16_ref, k16_ref, v16_ref,
                     kt_ref, vt_ref, u_ref, z_ref, z4_ref_all, *, tm):
    h = _rms(x_ref[...], g_ref[...]).astype(BF16)
    a = _dot(h, w_ref[:, 3 * ATTN_W:3 * ATTN_W + CONV_CH])
    gate = _dot(h, w_ref[:, 3 * ATTN_W + CONV_CH:3 * ATTN_W + 2 * CONV_CH])
    u_ref[...] = a * _sigmoid(gate)

    def project(which, scale, tok_ref, d4_ref, d16_ref, t_ref):
        z = _dot(h, w_ref[:, which * ATTN_W:(which + 1) * ATTN_W])
        if scale != 1.0:
            z = z * scale
        tok_ref[...] = z.astype(BF16)
        if t_ref is not None:
            t_ref[...] = z.T
        zs_ref, z4_ref = z_ref.at[which], z4_ref_all.at[which]
        for s in range(N_SLABS):
            ls = slice(s * LANES, (s + 1) * LANES)
            zs_ref[s] = z[:, ls]
            for r in range(4):
                z4 = zs_ref[s, pl.ds(r, tm // 4, stride=4), :]
                d4_ref[r, :, ls] = z4.astype(BF16)
                z4_ref[s, r * (tm // 4):(r + 1) * (tm // 4), :] = z4
            for r in range(16):
                z16 = z4_ref[s, pl.ds((r % 4) * (tm // 4) + r // 4, tm // 16, stride=4), :]
                d16_ref[r, :, ls] = z16.astype(BF16)

    project(0, HEAD_DIM ** -0.5, q1_ref, q4_ref, q16_ref, None)
    project(1, 1.0, k1_ref, k4_ref, k16_ref, kt_ref)
    project(2, 1.0, v1_ref, v4_ref, v16_ref, vt_ref)


def _p_inproj(x, g, w_in, tm):
    B, S, _ = x.shape
    W = min(MAX_WINDOW, S)
    nt = S // tm
    first = (S - W) // tm
    tok = lambda c: pl.BlockSpec((None, tm, c), lambda b, j: (b, j, 0))
    cls = lambda d: pl.BlockSpec((None, d, tm // d, ATTN_W), lambda b, j: (b, 0, j, 0))
    win = pl.BlockSpec((None, ATTN_W, tm), lambda b, j: (b, 0, jnp.maximum(j - first, 0)))
    bf = lambda *shape: jax.ShapeDtypeStruct(shape, BF16)
    consts = [g, w_in]
    return pl.pallas_call(
        functools.partial(_p_inproj_kernel, tm=tm),
        grid=(B, nt),
        in_specs=[tok(D_MODEL)] + [_const_spec(a.shape) for a in consts],
        out_specs=[tok(ATTN_W)] * 3 + [cls(4)] * 3 + [cls(16)] * 3 + [win, win, tok(CONV_CH)],
        out_shape=[bf(B, S, ATTN_W)] * 3 + [bf(B, 4, S // 4, ATTN_W)] * 3 + [bf(B, 16, S // 16, ATTN_W)] * 3
        + [jax.ShapeDtypeStruct((B, ATTN_W, W), F32)] * 2 + [jax.ShapeDtypeStruct((B, S, CONV_CH), F32)],
        scratch_shapes=[pltpu.VMEM((3, N_SLABS, tm, LANES), F32)] * 2,
        compiler_params=_params(("arbitrary", "arbitrary")),
        name="p_inproj",
    )(x, *consts)


def _p_conv_kernel(u_ref, halo_ref, cw_ref, cb_ref, c_ref, ext_ref, *, tm):
    j = pl.program_id(1)
    for s in range(N_SLABS):
        ls = slice(s * LANES, (s + 1) * LANES)
        ext_ref[s, 0:HALO, :] = jnp.where(j > 0, halo_ref[:, ls], 0.0)
        ext_ref[s, HALO:HALO + tm, :] = u_ref[:, ls]
    off = HALO - (CONV_K - 1)
    for s in range(N_SLABS):
        ls = slice(s * LANES, (s + 1) * LANES)
        for r0 in range(0, tm, CONV_ROWS):
            acc = jnp.zeros((CONV_ROWS, LANES), F32)
            for t in range(CONV_K):
                acc = acc + cw_ref[t:t + 1, ls] * ext_ref[s, r0 + off + t:r0 + off + t + CONV_ROWS, :]
            c_ref[r0:r0 + CONV_ROWS, ls] = acc + cb_ref[:, ls]


def _p_conv(u, cw, cb, tm):
    B, S, _ = u.shape
    tok = pl.BlockSpec((None, tm, CONV_CH), lambda b, j: (b, j, 0))
    halo = pl.BlockSpec((None, HALO, CONV_CH), lambda b, j: (b, jnp.maximum(j * (tm // HALO) - 1, 0), 0))
    consts = [cw, cb]
    return pl.pallas_call(
        functools.partial(_p_conv_kernel, tm=tm),
        grid=(B, S // tm),
        in_specs=[tok, halo] + [_const_spec(a.shape) for a in consts],
        out_specs=tok,
        out_shape=jax.ShapeDtypeStruct((B, S, CONV_CH), F32),
        scratch_shapes=[pltpu.VMEM((N_SLABS, HALO + tm, LANES), F32)],
        compiler_params=_params(("arbitrary", "arbitrary")),
        name="p_conv",
    )(u, u, *consts)


SM_ROWS = 32


def _pair_scores(qp, k_parts, v_parts, bias_ref, s_ref, pb_ref, mb_ref, lb_ref):
    lo = lax.broadcasted_iota(jnp.int32, (BLK, LANES), 1) < HEAD_DIM
    los = lax.broadcasted_iota(jnp.int32, (SM_ROWS, LANES), 1) < HEAD_DIM
    zq = jnp.zeros_like(qp)
    q2 = jnp.concatenate([jnp.where(lo, qp, zq), jnp.where(lo, zq, qp)], axis=0)
    s_ref[...] = _dot_nt(q2, jnp.concatenate(k_parts, axis=0))
    for r0 in range(0, BLK, SM_ROWS):
        rows = slice(r0, r0 + SM_ROWS)
        bias = bias_ref[rows, :]
        stats = []
        for hh in range(2):
            sc = s_ref[hh * BLK + r0:hh * BLK + r0 + SM_ROWS, :] + bias
            m = jnp.max(sc, axis=-1, keepdims=True)
            p = jnp.exp(sc - m)
            stats.append((m, jnp.sum(p, axis=-1, keepdims=True)))
            pb_ref[rows, hh * 2 * BLK:(hh + 1) * 2 * BLK] = p.astype(BF16)
        mb_ref[rows, :] = jnp.where(los, stats[0][0], stats[1][0])
        lb_ref[rows, :] = jnp.where(los, stats[0][1], stats[1][1])
    pv = []
    for hh in range(2):
        pv.append(sum(_dot(pb_ref[:, (2 * hh + part) * BLK:(2 * hh + part + 1) * BLK], vp)
                      for part, vp in enumerate(v_parts)))
    return jnp.where(lo, pv[0], pv[1])


def _p_attn_kernel(q16_ref, k16_ref, k16p_ref, v16_ref, v16p_ref,
                   q4_ref, k4_ref, k4p_ref, v4_ref, v4p_ref,
                   q1_ref, k1_ref, k1p_ref, v1_ref, v1p_ref,
                   o_ref, m_ref, l_ref, a_ref, bias_ref, s_ref, pb_ref, mb_ref, lb_ref):
    c = pl.program_id(1)
    ph = pl.program_id(2)

    @pl.when((c == 0) & (ph == 0))
    def _():
        row = lax.broadcasted_iota(jnp.int32, (BLK, 2 * BLK), 0)
        col = lax.broadcasted_iota(jnp.int32, (BLK, 2 * BLK), 1)
        band = (col >= row) & (col <= row + BLK)
        bias_ref[0] = jnp.where(band, 0.0, NEG)
        bias_ref[1] = jnp.where(band & (col >= BLK), 0.0, NEG)

    def unit(qp, kk, vv, seq_start, state_rows, first, final, out_rows=None):
        pair, rows = state_rows
        bias = bias_ref.at[jnp.where(seq_start, 1, 0)] if not isinstance(seq_start, bool) else bias_ref.at[int(seq_start)]
        buf = pair % 2
        pv = _pair_scores(qp, kk, vv, bias, s_ref.at[buf], pb_ref.at[buf], mb_ref.at[buf], lb_ref.at[buf])
        m_b = mb_ref[buf]
        l_b = lb_ref[buf]
        if first:
            m_new, l_new, a_new = m_b, l_b, pv
        else:
            m0 = m_ref[pair, rows, :]
            m_new = jnp.maximum(m0, m_b)
            a0 = jnp.exp(m0 - m_new)
            a1 = jnp.exp(m_b - m_new)
            l_new = a0 * l_ref[pair, rows, :] + a1 * l_b
            a_new = a0 * a_ref[pair, rows, :] + a1 * pv
        if final:
            o_ref[out_rows, pair * LANES:(pair + 1) * LANES] = (a_new / l_new).astype(o_ref.dtype)
        else:
            m_ref[pair, rows, :] = m_new
            l_ref[pair, rows, :] = l_new
            a_ref[pair, rows, :] = a_new

    @pl.when(ph < PHASES_PER_BRANCH)
    def _():
        for cl in range(4):
            r = ph * 4 + cl
            for pair in range(N_SLABS):
                ls = slice(pair * LANES, (pair + 1) * LANES)
                kk = (k16p_ref[cl, :, ls], k16_ref[cl, :, ls])
                vv = (v16p_ref[cl, :, ls], v16_ref[cl, :, ls])
                unit(q16_ref[cl, :, ls], kk, vv, c == 0, (pair, pl.ds(r, BLK, stride=16)), first=True, final=False)

    def contiguous_branch(q_ref, k_ref, kp_ref, v_ref, vp_ref, seq_start, rows_of, final):
        for n in range(TQ // BLK):
            for pair in range(N_SLABS):
                ls = slice(pair * LANES, (pair + 1) * LANES)
                if n == 0:
                    kk = (kp_ref[:, ls], k_ref[0:BLK, ls])
                    vv = (vp_ref[:, ls], v_ref[0:BLK, ls])
                else:
                    kk = (k_ref[(n - 1) * BLK:n * BLK, ls], k_ref[n * BLK:(n + 1) * BLK, ls])
                    vv = (v_ref[(n - 1) * BLK:n * BLK, ls], v_ref[n * BLK:(n + 1) * BLK, ls])
                unit(q_ref[n * BLK:(n + 1) * BLK, ls], kk, vv, seq_start if n == 0 else False, (pair, rows_of(n)),
                     first=False, final=final, out_rows=slice(n * BLK, (n + 1) * BLK))

    @pl.when((ph >= PHASES_PER_BRANCH) & (ph < 2 * PHASES_PER_BRANCH))
    def _():
        r = ph - PHASES_PER_BRANCH
        contiguous_branch(q4_ref, k4_ref, k4p_ref, v4_ref, v4p_ref, c == 0,
                          lambda n: pl.ds(4 * BLK * n + r, BLK, stride=4), final=False)

    @pl.when(ph >= 2 * PHASES_PER_BRANCH)
    def _():
        i = ph - 2 * PHASES_PER_BRANCH
        contiguous_branch(q1_ref, k1_ref, k1p_ref, v1_ref, v1p_ref, (c == 0) & (i == 0),
                          lambda n: pl.ds(pl.multiple_of(i * TQ + n * BLK, BLK), BLK), final=True)


def _p_attn(q1, k1, v1, q4, k4, v4, q16, k16, v16):
    B, S, _ = q1.shape
    P = PHASES_PER_BRANCH
    assert S % CHUNK == 0 and TQ % BLK == 0
    d16 = pl.BlockSpec((None, 4, BLK, ATTN_W), lambda b, c, p: (b, _clip(p, P - 1), c, 0))
    d16p = pl.BlockSpec((None, 4, BLK, ATTN_W), lambda b, c, p: (b, _clip(p, P - 1), jnp.maximum(c - 1, 0), 0))
    d4 = pl.BlockSpec((None, None, TQ, ATTN_W), lambda b, c, p: (b, _clip(p - P, P - 1), c, 0))
    d4p = pl.BlockSpec((None, None, BLK, ATTN_W),
                       lambda b, c, p: (b, _clip(p - P, P - 1), jnp.maximum(c * (TQ // BLK) - 1, 0), 0))
    d1 = pl.BlockSpec((None, TQ, ATTN_W), lambda b, c, p: (b, c * P + _clip(p - 2 * P, P - 1), 0))
    d1p = pl.BlockSpec((None, BLK, ATTN_W),
                       lambda b, c, p: (b, jnp.maximum((c * P + _clip(p - 2 * P, P - 1)) * (TQ // BLK) - 1, 0), 0))
    return pl.pallas_call(
        _p_attn_kernel,
        grid=(B, S // CHUNK, 3 * P),
        in_specs=[d16, d16, d16p, d16, d16p, d4, d4, d4p, d4, d4p, d1, d1, d1p, d1, d1p],
        out_specs=d1,
        out_shape=jax.ShapeDtypeStruct((B, S, ATTN_W), BF16),
        scratch_shapes=[pltpu.VMEM((N_SLABS, CHUNK, LANES), F32)] * 3
        + [pltpu.VMEM((2, BLK, 2 * BLK), F32), pltpu.VMEM((2, 2 * BLK, 2 * BLK), F32),
           pltpu.VMEM((2, BLK, 4 * BLK), BF16), pltpu.VMEM((2, BLK, LANES), F32), pltpu.VMEM((2, BLK, LANES), F32)],
        compiler_params=_params(("arbitrary",) * 3),
        name="p_attn",
    )(q16, k16, k16, v16, v16, q4, k4, k4, v4, v4, q1, k1, k1, v1, v1)


def _p_memkv_kernel(mem_ref, g_ref, wk_ref, wv_ref, k_ref, v_ref, kb_ref, vb_ref):
    h = _rms(mem_ref[...], g_ref[...]).astype(BF16)
    k = _dot(h, wk_ref[...])
    v = _dot(h, wv_ref[...])
    kb_ref[...] = k.astype(BF16)
    vb_ref[...] = v.astype(BF16)
    for hd in range(N_XHEADS):
        k_ref[:, hd, :] = k[:, hd * XHEAD_DIM:(hd + 1) * XHEAD_DIM]
        v_ref[:, hd, :] = v[:, hd * XHEAD_DIM:(hd + 1) * XHEAD_DIM]


def _p_memkv(mem, g, wk, wv):
    B, M, _ = mem.shape
    blk = pl.BlockSpec((None, M, D_MODEL), lambda b: (b, 0, 0))
    oblk = pl.BlockSpec((None, M, N_XHEADS, XHEAD_DIM), lambda b: (b, 0, 0, 0))
    return pl.pallas_call(
        _p_memkv_kernel,
        grid=(B,),
        in_specs=[blk, _const_spec(g.shape), _const_spec(wk.shape), _const_spec(wv.shape)],
        out_specs=[oblk, oblk, blk, blk],
        out_shape=[jax.ShapeDtypeStruct((B, M, N_XHEADS, XHEAD_DIM), F32)] * 2
        + [jax.ShapeDtypeStruct((B, M, D_MODEL), BF16)] * 2,
        compiler_params=_params(("arbitrary",)),
        name="p_memkv",
    )(mem, g, wk, wv)


def _p_post_kernel(attn_ref, c_ref, x_ref, mk_ref, mv_ref, lng_ref, lnb_ref, goc_ref, goa_ref, wout_ref, gpm_ref,
                   gpre_ref, wq_ref, wo_ref, gpost_ref, o_ref):
    ma = _rms(attn_ref[...].astype(F32), goa_ref[...]).astype(BF16)
    mc = _rms(_conv_post(c_ref[...], lng_ref[...], lnb_ref[...]), goc_ref[...]).astype(BF16)
    y = _dot(ma, wout_ref[0:ATTN_W, :]) + _dot(mc, wout_ref[ATTN_W:D_MODEL, :])
    x1 = x_ref[...] + _rms(y, gpm_ref[...])
    h = _rms(x1, gpre_ref[...]).astype(BF16)
    qx = (_dot(h, wq_ref[...]) * (XHEAD_DIM ** -0.5)).astype(BF16)
    ox = _xattn_heads(qx, mk_ref, mv_ref)
    o_ref[...] = x1 + _rms(_dot(ox, wo_ref[...]), gpost_ref[...])


def _p_post(attn, c, x, mk, mv, lng, lnb, goc, goa, wout, gpm, gpre, wq, wo, gpost, tm):
    B, S, _ = x.shape
    tok = lambda c: pl.BlockSpec((None, tm, c), lambda b, j: (b, j, 0))
    mem = pl.BlockSpec((None, N_MEM, D_MODEL), lambda b, j: (b, 0, 0))
    consts = [lng, lnb, goc, goa, wout, gpm, gpre, wq, wo, gpost]
    return pl.pallas_call(
        _p_post_kernel,
        grid=(B, S // tm),
        in_specs=[tok(ATTN_W), tok(CONV_CH), tok(D_MODEL), mem, mem] + [_const_spec(a.shape) for a in consts],
        out_specs=tok(D_MODEL),
        out_shape=jax.ShapeDtypeStruct((B, S, D_MODEL), F32),
        compiler_params=_params(("arbitrary", "arbitrary")),
        name="p_post",
    )(attn, c, x, mk, mv, *consts)


FFN_PAD = 8


def _p_ffn_kernel(x_ref, gpre_ref, wup_ref, cw_ref, wdn_ref, gpost_ref, y_ref, st_ref, ext_ref, g_ref, *, tm):
    j = pl.program_id(1)

    @pl.when(j == 0)
    def _():
        ext_ref[0:FFN_PAD, :] = jnp.zeros((FFN_PAD, D_FF), F32)

    x = x_ref[...]
    h = _rms(x, gpre_ref[...]).astype(BF16)
    for c0 in range(0, D_FF, FF_CHUNK):
        cs = slice(c0, c0 + FF_CHUNK)
        ext_ref[FFN_PAD:FFN_PAD + tm, cs] = _dot(h, wup_ref[:, cs])
        lin = _dot(h, wup_ref[:, D_FF + c0:D_FF + c0 + FF_CHUNK])
        f = (cw_ref[0:1, cs] * ext_ref[FFN_PAD - 2:FFN_PAD - 2 + tm, cs]
             + cw_ref[1:2, cs] * ext_ref[FFN_PAD - 1:FFN_PAD - 1 + tm, cs]
             + cw_ref[2:3, cs] * ext_ref[FFN_PAD:FFN_PAD + tm, cs])
        g_ref[:, cs] = _ffn_act(f, lin)
    tail = ext_ref[tm:tm + FFN_PAD, :]
    ext_ref[0:FFN_PAD, :] = tail
    st_ref[...] = tail[FFN_PAD - (FFN_K - 1):FFN_PAD, :]
    y_ref[...] = x + _rms(_dot(g_ref[...], wdn_ref[...]), gpost_ref[...])


def _p_ffn(x, gpre, wup, cw, wdn, gpost, tm):
    B, S, _ = x.shape
    tok = pl.BlockSpec((None, tm, D_MODEL), lambda b, j: (b, j, 0))
    st = pl.BlockSpec((None, FFN_K - 1, D_FF), lambda b, j: (b, 0, 0))
    return pl.pallas_call(
        functools.partial(_p_ffn_kernel, tm=tm),
        grid=(B, S // tm),
        in_specs=[tok, _const_spec(gpre.shape), _const_spec(wup.shape), _const_spec(cw.shape),
                  _const_spec(wdn.shape), _const_spec(gpost.shape)],
        out_specs=[tok, st],
        out_shape=[jax.ShapeDtypeStruct((B, S, D_MODEL), F32), jax.ShapeDtypeStruct((B, FFN_K - 1, D_FF), F32)],
        scratch_shapes=[pltpu.VMEM((FFN_PAD + tm, D_FF), F32), pltpu.VMEM((tm, D_FF), BF16)],
        compiler_params=_params(("arbitrary", "arbitrary")),
        name="p_ffn",
    )(x, gpre, wup, cw, wdn, gpost)


def _tcol(t, c):
    return slice(t * c, (t + 1) * c)


def _single_step_call(kernel, ins, out_shapes, name, scratch=()):
    return pl.pallas_call(
        kernel,
        grid=(1,),
        in_specs=[_const_spec(a.shape) for a in ins],
        out_specs=[pl.BlockSpec(s.shape, lambda *_, nd=len(s.shape): (0,) * nd) for s in out_shapes],
        out_shape=out_shapes,
        scratch_shapes=list(scratch),
        compiler_params=_params(("arbitrary",)),
        name=name,
    )(*ins)


def _s_inproj_kernel(x_ref, g_ref, w_ref, q_ref, k_ref, v_ref, kt_ref, vt_ref, u_ref):
    bd = x_ref.shape[0]
    g = g_ref[...]
    h = jnp.concatenate([_rms(x_ref[:, t, :], g).astype(BF16) for t in range(T_NEW)], axis=0)
    q = _dot(h, w_ref[:, 0:ATTN_W]) * (HEAD_DIM ** -0.5)
    k = _dot(h, w_ref[:, ATTN_W:2 * ATTN_W])
    v = _dot(h, w_ref[:, 2 * ATTN_W:3 * ATTN_W])
    a = _dot(h, w_ref[:, 3 * ATTN_W:3 * ATTN_W + CONV_CH])
    gate = _dot(h, w_ref[:, 3 * ATTN_W + CONV_CH:3 * ATTN_W + 2 * CONV_CH])
    u = a * _sigmoid(gate)
    for t in range(T_NEW):
        rows = slice(t * bd, (t + 1) * bd)
        q_ref[:, _tcol(t, ATTN_W)] = q[rows]
        k_ref[:, _tcol(t, ATTN_W)] = k[rows]
        v_ref[:, _tcol(t, ATTN_W)] = v[rows]
        kt_ref[t] = k[rows].T
        vt_ref[t] = v[rows].T
        u_ref[:, _tcol(t, CONV_CH)] = u[rows]


def _s_inproj(x, g, w_in):
    bd = x.shape[0]
    flat = jax.ShapeDtypeStruct((bd, T_NEW * ATTN_W), F32)
    tr = jax.ShapeDtypeStruct((T_NEW, ATTN_W, bd), F32)
    return _single_step_call(_s_inproj_kernel, [x, g, w_in], [flat, flat, flat, tr, tr, flat], "s_inproj")


SB = 8
NEW_PAD = 8


def _sample_key_multiplicity():
    t = np.arange(T_NEW)[:, None]
    dist = MAX_WINDOW + t - np.arange(MAX_WINDOW)[None, :]
    mult = np.zeros((T_NEW, MAX_WINDOW), np.float32)
    for d in DILATIONS:
        mult += (dist % d == 0) & (dist // d >= 1) & (dist // d <= BLK)
    tn = np.arange(NEW_PAD)[None, :]
    new = np.where(tn == t, float(len(DILATIONS)), np.where(tn < t, 1.0, 0.0)).astype(np.float32)
    return np.repeat(mult, N_HEADS, axis=0), np.repeat(new, N_HEADS, axis=0)


def _s_attn_kernel(q_ref, kn_ref, vn_ref, kt_ref, vt_ref, mw_ref, mn_ref, o_ref):
    ii = pl.program_id(1)
    rows8 = lax.broadcasted_iota(jnp.int32, (SB, ATTN_W), 0)
    sel = rows8 == ii

    @pl.when(ii == 0)
    def _():
        o_ref[...] = jnp.zeros(o_ref.shape, F32)

    def pick(ref, t):
        return jnp.sum(jnp.where(sel, ref[:, _tcol(t, ATTN_W)], 0.0), axis=0, keepdims=True)

    head_of_lane = lax.broadcasted_iota(jnp.int32, (N_HEADS, ATTN_W), 1) // HEAD_DIM
    diag = head_of_lane == lax.broadcasted_iota(jnp.int32, (N_HEADS, ATTN_W), 0)
    qbd = jnp.concatenate([jnp.where(diag, pick(q_ref, t), 0.0) for t in range(T_NEW)], axis=0).astype(BF16)
    pad = jnp.zeros((NEW_PAD - T_NEW, ATTN_W), F32)
    k_new = jnp.concatenate([pick(kn_ref, t) for t in range(T_NEW)] + [pad], axis=0).astype(BF16)
    v_new = jnp.concatenate([pick(vn_ref, t) for t in range(T_NEW)] + [pad], axis=0).astype(BF16)

    kt = kt_ref[...].reshape(ATTN_W, MAX_WINDOW).astype(BF16)
    vt = vt_ref[...].reshape(ATTN_W, MAX_WINDOW).astype(BF16)
    mw = mw_ref[...]
    mn = mn_ref[...]
    s_w = jnp.where(mw > 0.0, _dot(qbd, kt), NEG)
    s_n = jnp.where(mn > 0.0, _dot_nt(qbd, k_new), NEG)
    m = jnp.maximum(jnp.max(s_w, axis=-1, keepdims=True), jnp.max(s_n, axis=-1, keepdims=True))
    p_w = jnp.exp(s_w - m) * mw
    p_n = jnp.exp(s_n - m) * mn
    l = jnp.sum(p_w, axis=-1, keepdims=True) + jnp.sum(p_n, axis=-1, keepdims=True)
    o_full = _dot_nt((p_w / l).astype(BF16), vt) + _dot((p_n / l).astype(BF16), v_new)
    for t in range(T_NEW):
        o = jnp.sum(jnp.where(diag, o_full[t * N_HEADS:(t + 1) * N_HEADS], 0.0), axis=0, keepdims=True)
        cur = o_ref[:, _tcol(t, ATTN_W)]
        o_ref[:, _tcol(t, ATTN_W)] = jnp.where(sel, o, cur)


def _s_attn(q, kn, vn, win_kt, win_vt):
    bd = q.shape[0]
    mw, mn = _sample_key_multiplicity()
    rowblk = pl.BlockSpec((SB, T_NEW * ATTN_W), lambda g, i: (g, 0))
    win = pl.BlockSpec((None, N_HEADS, HEAD_DIM, MAX_WINDOW), lambda g, i: (g * SB + i, 0, 0, 0))
    return pl.pallas_call(
        _s_attn_kernel,
        grid=(bd // SB, SB),
        in_specs=[rowblk, rowblk, rowblk, win, win, _const_spec(mw.shape), _const_spec(mn.shape)],
        out_specs=rowblk,
        out_shape=jax.ShapeDtypeStruct((bd, T_NEW * ATTN_W), F32),
        compiler_params=_params(("arbitrary", "arbitrary")),
        name="s_attn",
    )(q, kn, vn, win_kt, win_vt, jnp.asarray(mw), jnp.asarray(mn))


def _s_mix_kernel(attn_ref, u_ref, st_ref, x_ref, cw_ref, cb_ref, lng_ref, lnb_ref, goa_ref, goc_ref,
                  wout_ref, gpm_ref, gpre_ref, wq_ref, x1_ref, qx_ref, nst_ref):
    bd = x_ref.shape[0]
    nctx = CONV_K - 1

    def ctx(r):
        return st_ref[r] if r < nctx else u_ref[:, _tcol(r - nctx, CONV_CH)]

    mcs, mas = [], []
    for t in range(T_NEW):
        acc = jnp.zeros((bd, CONV_CH), F32)
        for j in range(CONV_K):
            acc = acc + cw_ref[j:j + 1, :] * ctx(t + j)
        c = _conv_post(acc + cb_ref[...], lng_ref[...], lnb_ref[...])
        mcs.append(_rms(c, goc_ref[...]).astype(BF16))
        mas.append(_rms(attn_ref[:, _tcol(t, ATTN_W)], goa_ref[...]).astype(BF16))
    y = (_dot(jnp.concatenate(mas, axis=0), wout_ref[0:ATTN_W, :])
         + _dot(jnp.concatenate(mcs, axis=0), wout_ref[ATTN_W:D_MODEL, :]))
    hs = []
    for t in range(T_NEW):
        x1 = x_ref[:, t, :] + _rms(y[t * bd:(t + 1) * bd], gpm_ref[...])
        x1_ref[:, _tcol(t, D_MODEL)] = x1
        hs.append(_rms(x1, gpre_ref[...]).astype(BF16))
    q = _dot(jnp.concatenate(hs, axis=0), wq_ref[...]) * (XHEAD_DIM ** -0.5)
    for t in range(T_NEW):
        qx_ref[:, _tcol(t, D_MODEL)] = q[t * bd:(t + 1) * bd]
    for r in range(nctx):
        nst_ref[r] = ctx(r + T_NEW)


def _s_mix(attn, u, state, x, cw, cb, lng, lnb, goa, goc, wout, gpm, gpre, wq):
    bd = x.shape[0]
    full = jax.ShapeDtypeStruct((bd, T_NEW * D_MODEL), F32)
    outs = [full, full, jax.ShapeDtypeStruct(state.shape, F32)]
    ins = [attn, u, state, x, cw, cb, lng, lnb, goa, goc, wout, gpm, gpre, wq]
    return _single_step_call(_s_mix_kernel, ins, outs, "s_mix")


def _s_xcore_kernel(q_ref, mk_ref, mv_ref, o_ref):
    ii = pl.program_id(1)
    rows8 = lax.broadcasted_iota(jnp.int32, (SB, D_MODEL), 0)
    sel = rows8 == ii

    @pl.when(ii == 0)
    def _():
        o_ref[...] = jnp.zeros(o_ref.shape, F32)

    nq = T_NEW * N_XHEADS
    qrow = lax.broadcasted_iota(jnp.int32, (nq, XHEAD_DIM), 0)
    q16 = jnp.zeros((nq, XHEAD_DIM), F32)
    for t in range(T_NEW):
        qt = jnp.sum(jnp.where(sel, q_ref[:, _tcol(t, D_MODEL)], 0.0), axis=0, keepdims=True)
        for h in range(N_XHEADS):
            q16 = jnp.where(qrow == t * N_XHEADS + h, qt[:, _tcol(h, XHEAD_DIM)], q16)
    k2 = mk_ref[...].reshape(N_MEM * N_XHEADS, XHEAD_DIM).astype(BF16)
    v2 = mv_ref[...].reshape(N_MEM * N_XHEADS, XHEAD_DIM).astype(BF16)
    srow = lax.broadcasted_iota(jnp.int32, (nq, N_MEM * N_XHEADS), 0)
    scol = lax.broadcasted_iota(jnp.int32, (nq, N_MEM * N_XHEADS), 1)
    s = jnp.where(srow % N_XHEADS == scol % N_XHEADS, _dot_nt(q16.astype(BF16), k2), NEG)
    m = jnp.max(s, axis=-1, keepdims=True)
    p = jnp.exp(s - m)
    l = jnp.sum(p, axis=-1, keepdims=True)
    o16 = _dot((p / l).astype(BF16), v2)
    for t in range(T_NEW):
        ot = jnp.concatenate([o16[t * N_XHEADS + h:t * N_XHEADS + h + 1, :] for h in range(N_XHEADS)], axis=1)
        cur = o_ref[:, _tcol(t, D_MODEL)]
        o_ref[:, _tcol(t, D_MODEL)] = jnp.where(sel, ot, cur)


def _s_xcore(q, mem_k, mem_v):
    bd = q.shape[0]
    rowblk = pl.BlockSpec((SB, T_NEW * D_MODEL), lambda g, i: (g, 0))
    mem = pl.BlockSpec((None, N_MEM, N_XHEADS, XHEAD_DIM), lambda g, i: (g * SB + i, 0, 0, 0))
    return pl.pallas_call(
        _s_xcore_kernel,
        grid=(bd // SB, SB),
        in_specs=[rowblk, mem, mem],
        out_specs=rowblk,
        out_shape=jax.ShapeDtypeStruct((bd, T_NEW * D_MODEL), F32),
        compiler_params=_params(("arbitrary", "arbitrary")),
        name="s_xcore",
    )(q, mem_k, mem_v)


def _s_ffn_kernel(ox_ref, x1_ref, st_ref, wo_ref, gqx_ref, gpre_ref, wup_ref, cw_ref, wdn_ref, gpost_ref,
                  y_ref, nst_ref, g_ref):
    bd = x1_ref.shape[0]
    ox = jnp.concatenate([ox_ref[:, _tcol(t, D_MODEL)].astype(BF16) for t in range(T_NEW)], axis=0)
    yx = _dot(ox, wo_ref[...])
    x2 = [x1_ref[:, _tcol(t, D_MODEL)] + _rms(yx[t * bd:(t + 1) * bd], gqx_ref[...]) for t in range(T_NEW)]
    h = jnp.concatenate([_rms(x2[t], gpre_ref[...]).astype(BF16) for t in range(T_NEW)], axis=0)
    for c0 in range(0, D_FF, FF_CHUNK):
        cs = slice(c0, c0 + FF_CHUNK)
        act = _dot(h, wup_ref[:, cs])
        lin = _dot(h, wup_ref[:, D_FF + c0:D_FF + c0 + FF_CHUNK])
        a = [st_ref[:, r, cs] for r in range(FFN_K - 1)]
        a += [act[t * bd:(t + 1) * bd] for t in range(T_NEW)]
        for t in range(T_NEW):
            f = cw_ref[0:1, cs] * a[t] + cw_ref[1:2, cs] * a[t + 1] + cw_ref[2:3, cs] * a[t + 2]
            g_ref[t * bd:(t + 1) * bd, cs] = _ffn_act(f, lin[t * bd:(t + 1) * bd])
        for r in range(FFN_K - 1):
            nst_ref[:, r, cs] = a[T_NEW + r]
    y = _dot(g_ref[...], wdn_ref[...])
    for t in range(T_NEW):
        y_ref[:, t, :] = x2[t] + _rms(y[t * bd:(t + 1) * bd], gpost_ref[...])


def _s_ffn(ox, x1, state, wo, gqx, gpre, wup, cw, wdn, gpost):
    bd = x1.shape[0]
    outs = [jax.ShapeDtypeStruct((bd, T_NEW, D_MODEL), F32), jax.ShapeDtypeStruct(state.shape, F32)]
    return _single_step_call(_s_ffn_kernel, [ox, x1, state, wo, gqx, gpre, wup, cw, wdn, gpost], outs, "s_ffn",
                             scratch=[pltpu.VMEM((T_NEW * bd, D_FF), BF16)])


def kernel(x_prompt, x_sample, mem_prompt, cache_win_k, cache_win_v, state_conv, state_ffn_conv, cache_mem_k, cache_mem_v, g_pre_mix, w_in, conv_w, conv_b, ln_conv_g, ln_conv_b, g_out_attn, g_out_conv, w_out, g_post_mix, g_pre_x, g_mem, w_xq, w_mk, w_mv, w_xo, g_post_x, g_pre_ffn, w_up, ffn_conv_w, w_down, g_post_ffn):
    depth = w_in.shape[0]
    assert depth == 1
    B, S, _ = x_prompt.shape
    Bd, T, _ = x_sample.shape
    assert T == T_NEW and cache_win_k.shape[2] == MAX_WINDOW and S % CHUNK == 0 and Bd % SB == 0

    gpm, gpx, gm, gqx, gpf, gqf = (_row(a[0]) for a in (g_pre_mix, g_pre_x, g_mem, g_post_x, g_pre_ffn, g_post_ffn))
    gpo, goa, goc = _row(g_post_mix[0]), _row(g_out_attn[0]), _row(g_out_conv[0])
    cb, lng, lnb = _row(conv_b[0]), _row(ln_conv_g[0]), _row(ln_conv_b[0])
    cw, fcw = conv_w[0], ffn_conv_w[0]
    win, wout, wxq, wmk, wmv, wxo, wup, wdn = (
        a[0].astype(BF16) for a in (w_in, w_out, w_xq, w_mk, w_mv, w_xo, w_up, w_down))
    W = min(MAX_WINDOW, S)

    (q1, k1, v1, q4, k4, v4, q16, k16, v16, kt_p, vt_p, u_p) = _p_inproj(x_prompt, gpm, win, tm=512)
    c_p = _p_conv(u_p, cw, cb, tm=512)
    attn = _p_attn(q1, k1, v1, q4, k4, v4, q16, k16, v16)
    mk_p, mv_p, mkb, mvb = _p_memkv(mem_prompt, gm, wmk, wmv)
    x2 = _p_post(attn, c_p, x_prompt, mkb, mvb, lng, lnb, goc, goa, wout, gpo, gpx, wxq, wxo, gqx, tm=512)
    y_p, ffn_p = _p_ffn(x2, gpf, wup, fcw, wdn, gqf, tm=512)
    conv_p = u_p[:, S - (CONV_K - 1):, :]
    to_win = lambda t: jnp.transpose(t.reshape(1, B, N_HEADS, HEAD_DIM, W), (0, 1, 4, 2, 3))

    qs, ks, vs, kt_s, vt_s, us = _s_inproj(x_sample, gpm, win)
    win_kt = jnp.transpose(cache_win_k[0], (0, 2, 3, 1))
    win_vt = jnp.transpose(cache_win_v[0], (0, 2, 3, 1))
    attn_s = _s_attn(qs, ks, vs, win_kt, win_vt)
    st_conv = jnp.transpose(state_conv[0], (1, 0, 2))
    x1s, qxs, conv_s = _s_mix(attn_s, us, st_conv, x_sample, cw, cb, lng, lnb, goa, goc, wout, gpo, gpx, wxq)
    oxs = _s_xcore(qxs, cache_mem_k[0], cache_mem_v[0])
    y_s, ffn_s = _s_ffn(oxs, x1s, state_ffn_conv[0], wxo, gqx, gpf, wup, fcw, wdn, gqf)
    to_rows = lambda t: jnp.transpose(t.reshape(1, T, N_HEADS, HEAD_DIM, Bd), (0, 4, 1, 2, 3))

    return (y_p, y_s, to_win(kt_p), to_win(vt_p), conv_p[None], ffn_p[None], mk_p[None], mv_p[None],
            to_rows(kt_s), to_rows(vt_s), jnp.transpose(conv_s, (1, 0, 2))[None], ffn_s[None])
```

```python
import functools
import math

import jax
import jax.numpy as jnp
import numpy as np
from jax import lax
from jax.experimental import pallas as pl
from jax.experimental.pallas import tpu as pltpu

F32 = jnp.float32
BF16 = jnp.bfloat16

EPS = 1e-6
D_MODEL = 1024
N_HEADS = 8
HEAD_DIM = 64
ATTN_W = N_HEADS * HEAD_DIM
CONV_CH = D_MODEL - ATTN_W
CONV_K = 31
BLK = 128
DILATIONS = (16, 4, 1)
MAX_WINDOW = 2048
N_MEM = 256
N_XHEADS = 4
XHEAD_DIM = D_MODEL // N_XHEADS
D_FF = 2816
FFN_K = 3
T_NEW = 4
NEG = -1e30
LANES = 128
N_SLABS = ATTN_W // LANES
FF_CHUNK = 256
VMEM_LIMIT = 56 * 1024 * 1024

CHUNK = 16 * BLK
PHASES_PER_BRANCH = 2
TQ = CHUNK // PHASES_PER_BRANCH
CLASSES16 = 16 // PHASES_PER_BRANCH
CLASSES4 = 4 // PHASES_PER_BRANCH
ROWS4 = CHUNK // 4


def _rms(x, g):
    return x * lax.rsqrt(jnp.mean(x * x, axis=-1, keepdims=True) + EPS) * g


def _dot(a, b):
    return jnp.dot(a, b, preferred_element_type=F32)


def _dot_nt(a, b):
    return lax.dot_general(a, b, (((1,), (1,)), ((), ())), preferred_element_type=F32)


def _sigmoid(x):
    return 1.0 / (1.0 + jnp.exp(-x))


def _gelu_tanh(x):
    c = math.sqrt(2.0 / math.pi)
    return x * (0.5 * (1.0 + jnp.tanh(c * (x + 0.044715 * (x * x * x)))))


def _conv_post(c, lng, lnb):
    mu = jnp.mean(c, axis=-1, keepdims=True)
    d = c - mu
    var = jnp.mean(d * d, axis=-1, keepdims=True)
    y = d * lax.rsqrt(var + EPS) * lng + lnb
    return y * _sigmoid(y)


def _ffn_act(f, lin):
    return (_gelu_tanh(f) * lin).astype(BF16)


def _xattn_heads(qx, mk_ref, mv_ref):
    outs = []
    for h in range(N_XHEADS):
        hs = slice(h * XHEAD_DIM, (h + 1) * XHEAD_DIM)
        s = _dot_nt(qx[:, hs], mk_ref[:, hs])
        m = jnp.max(s, axis=-1, keepdims=True)
        p = jnp.exp(s - m)
        l = jnp.sum(p, axis=-1, keepdims=True)
        outs.append(_dot((p / l).astype(BF16), mv_ref[:, hs]).astype(BF16))
    return jnp.concatenate(outs, axis=-1)


def _const_spec(shape):
    nd = len(shape)
    return pl.BlockSpec(shape, lambda *_: (0,) * nd, pipeline_mode=pl.Buffered(1))


def _params(sem):
    return pltpu.CompilerParams(dimension_semantics=sem, vmem_limit_bytes=VMEM_LIMIT)


def _row(v):
    return v.reshape(1, -1).astype(F32)


def _clip(v, hi):
    return jnp.minimum(jnp.maximum(v, 0), hi)


HALO = 32
CONV_ROWS = 64


def _p_inproj_kernel(x_ref, g_ref, w_ref,
                     q1_ref, k1_ref, v1_ref, q4_ref, k4_ref, v4_ref, q16_ref, k16_ref, v16_ref,
                     kt_ref, vt_ref, u_ref, z_ref, z4_ref_all, *, tm):
    h = _rms(x_ref[...], g_ref[...]).astype(BF16)
    a = _dot(h, w_ref[:, 3 * ATTN_W:3 * ATTN_W + CONV_CH])
    gate = _dot(h, w_ref[:, 3 * ATTN_W + CONV_CH:3 * ATTN_W + 2 * CONV_CH])
    u_ref[...] = a * _sigmoid(gate)

    def project(which, scale, tok_ref, d4_ref, d16_ref, t_ref):
        z = _dot(h, w_ref[:, which * ATTN_W:(which + 1) * ATTN_W])
        if scale != 1.0:
            z = z * scale
        tok_ref[...] = z.astype(BF16)
        if t_ref is not None:
            t_ref[...] = z.T
        zs_ref, z4_ref = z_ref.at[which], z4_ref_all.at[which]
        for s in range(N_SLABS):
            ls = slice(s * LANES, (s + 1) * LANES)
            zs_ref[s] = z[:, ls]
            for r in range(4):
                z4 = zs_ref[s, pl.ds(r, tm // 4, stride=4), :]
                d4_ref[r, :, ls] = z4.astype(BF16)
                z4_ref[s, r * (tm // 4):(r + 1) * (tm // 4), :] = z4
            for r in range(16):
                z16 = z4_ref[s, pl.ds((r % 4) * (tm // 4) + r // 4, tm // 16, stride=4), :]
                d16_ref[r, :, ls] = z16.astype(BF16)

    project(0, HEAD_DIM ** -0.5 * math.log2(math.e), q1_ref, q4_ref, q16_ref, None)
    project(1, 1.0, k1_ref, k4_ref, k16_ref, kt_ref)
    project(2, 1.0, v1_ref, v4_ref, v16_ref, vt_ref)


def _p_inproj(x, g, w_in, tm):
    B, S, _ = x.shape
    W = min(MAX_WINDOW, S)
    nt = S // tm
    first = (S - W) // tm
    tok = lambda c: pl.BlockSpec((None, tm, c), lambda b, j: (b, j, 0))
    cls = lambda d: pl.BlockSpec((None, d, tm // d, ATTN_W), lambda b, j: (b, 0, j, 0))
    win = pl.BlockSpec((None, ATTN_W, tm), lambda b, j: (b, 0, jnp.maximum(j - first, 0)))
    bf = lambda *shape: jax.ShapeDtypeStruct(shape, BF16)
    consts = [g, w_in]
    return pl.pallas_call(
        functools.partial(_p_inproj_kernel, tm=tm),
        grid=(B, nt),
        in_specs=[tok(D_MODEL)] + [_const_spec(a.shape) for a in consts],
        out_specs=[tok(ATTN_W)] * 3 + [cls(4)] * 3 + [cls(16)] * 3 + [win, win, tok(CONV_CH)],
        out_shape=[bf(B, S, ATTN_W)] * 3 + [bf(B, 4, S // 4, ATTN_W)] * 3 + [bf(B, 16, S // 16, ATTN_W)] * 3
        + [jax.ShapeDtypeStruct((B, ATTN_W, W), F32)] * 2 + [jax.ShapeDtypeStruct((B, S, CONV_CH), F32)],
        scratch_shapes=[pltpu.VMEM((3, N_SLABS, tm, LANES), F32)] * 2,
        compiler_params=_params(("arbitrary", "arbitrary")),
        name="p_inproj",
    )(x, *consts)


def _p_conv_kernel(u_ref, halo_ref, cw_ref, cb_ref, c_ref, ext_ref, *, tm):
    j = pl.program_id(1)
    for s in range(N_SLABS):
        ls = slice(s * LANES, (s + 1) * LANES)
        ext_ref[s, 0:HALO, :] = jnp.where(j > 0, halo_ref[:, ls], 0.0)
        ext_ref[s, HALO:HALO + tm, :] = u_ref[:, ls]
    off = HALO - (CONV_K - 1)
    for s in range(N_SLABS):
        ls = slice(s * LANES, (s + 1) * LANES)
        for r0 in range(0, tm, CONV_ROWS):
            acc = jnp.zeros((CONV_ROWS, LANES), F32)
            for t in range(CONV_K):
                acc = acc + cw_ref[t:t + 1, ls] * ext_ref[s, r0 + off + t:r0 + off + t + CONV_ROWS, :]
            c_ref[r0:r0 + CONV_ROWS, ls] = acc + cb_ref[:, ls]


def _p_conv(u, cw, cb, tm):
    B, S, _ = u.shape
    tok = pl.BlockSpec((None, tm, CONV_CH), lambda b, j: (b, j, 0))
    halo = pl.BlockSpec((None, HALO, CONV_CH), lambda b, j: (b, jnp.maximum(j * (tm // HALO) - 1, 0), 0))
    consts = [cw, cb]
    return pl.pallas_call(
        functools.partial(_p_conv_kernel, tm=tm),
        grid=(B, S // tm),
        in_specs=[tok, halo] + [_const_spec(a.shape) for a in consts],
        out_specs=tok,
        out_shape=jax.ShapeDtypeStruct((B, S, CONV_CH), F32),
        scratch_shapes=[pltpu.VMEM((N_SLABS, HALO + tm, LANES), F32)],
        compiler_params=_params(("arbitrary", "arbitrary")),
        name="p_conv",
    )(u, u, *consts)


SM_ROWS = 32


def _pair_scores(qp, k_parts, v_parts, bias_ref, s_ref, pb_ref, mb_ref, lb_ref):
    lo = lax.broadcasted_iota(jnp.int32, (BLK, LANES), 1) < HEAD_DIM
    los = lax.broadcasted_iota(jnp.int32, (SM_ROWS, LANES), 1) < HEAD_DIM
    zq = jnp.zeros_like(qp)
    q2 = jnp.concatenate([jnp.where(lo, qp, zq), jnp.where(lo, zq, qp)], axis=0)
    s_ref[...] = _dot_nt(q2, jnp.concatenate(k_parts, axis=0))
    for r0 in range(0, BLK, SM_ROWS):
        rows = slice(r0, r0 + SM_ROWS)
        bias = bias_ref[rows, :]
        stats = []
        for hh in range(2):
            sc = s_ref[hh * BLK + r0:hh * BLK + r0 + SM_ROWS, :] + bias
            m = jnp.max(sc, axis=-1, keepdims=True)
            p = jnp.exp2(sc - m)
            stats.append((m, jnp.sum(p, axis=-1, keepdims=True)))
            pb_ref[rows, hh * 2 * BLK:(hh + 1) * 2 * BLK] = p.astype(BF16)
        mb_ref[rows, :] = jnp.where(los, stats[0][0], stats[1][0])
        lb_ref[rows, :] = jnp.where(los, stats[0][1], stats[1][1])
    pv = []
    for hh in range(2):
        pv.append(sum(_dot(pb_ref[:, (2 * hh + part) * BLK:(2 * hh + part + 1) * BLK], vp)
                      for part, vp in enumerate(v_parts)))
    return jnp.where(lo, pv[0], pv[1])


def _p_attn_kernel(q16_ref, k16_ref, k16p_ref, v16_ref, v16p_ref,
                   q4_ref, k4_ref, k4p_ref, v4_ref, v4p_ref,
                   q1_ref, k1_ref, k1p_ref, v1_ref, v1p_ref,
                   o_ref, m_ref, l_ref, a_ref, bias_ref, s_ref, pb_ref, mb_ref, lb_ref):
    c = pl.program_id(1)
    ph = pl.program_id(2)

    @pl.when((c == 0) & (ph == 0))
    def _():
        row = lax.broadcasted_iota(jnp.int32, (BLK, 2 * BLK), 0)
        col = lax.broadcasted_iota(jnp.int32, (BLK, 2 * BLK), 1)
        band = (col >= row) & (col <= row + BLK)
        bias_ref[0] = jnp.where(band, 0.0, NEG)
        bias_ref[1] = jnp.where(band & (col >= BLK), 0.0, NEG)

    def unit(qp, kk, vv, seq_start, state_rows, first, final, out_rows=None):
        pair, rows = state_rows
        bias = bias_ref.at[jnp.where(seq_start, 1, 0)] if not isinstance(seq_start, bool) else bias_ref.at[int(seq_start)]
        buf = pair % 2
        pv = _pair_scores(qp, kk, vv, bias, s_ref.at[buf], pb_ref.at[buf], mb_ref.at[buf], lb_ref.at[buf])
        m_b = mb_ref[buf]
        l_b = lb_ref[buf]
        if first:
            m_new, l_new, a_new = m_b, l_b, pv
        else:
            m0 = m_ref[pair, rows, :]
            m_new = jnp.maximum(m0, m_b)
            a0 = jnp.exp2(m0 - m_new)
            a1 = jnp.exp2(m_b - m_new)
            l_new = a0 * l_ref[pair, rows, :] + a1 * l_b
            a_new = a0 * a_ref[pair, rows, :] + a1 * pv
        if final:
            o_ref[out_rows, pair * LANES:(pair + 1) * LANES] = (a_new / l_new).astype(o_ref.dtype)
        else:
            m_ref[pair, rows, :] = m_new
            l_ref[pair, rows, :] = l_new
            a_ref[pair, rows, :] = a_new

    @pl.when(ph < PHASES_PER_BRANCH)
    def _():
        for cl in range(CLASSES16):
            r = ph * CLASSES16 + cl
            for pair in range(N_SLABS):
                ls = slice(pair * LANES, (pair + 1) * LANES)
                kk = (k16p_ref[cl, :, ls], k16_ref[cl, :, ls])
                vv = (v16p_ref[cl, :, ls], v16_ref[cl, :, ls])
                unit(q16_ref[cl, :, ls], kk, vv, c == 0, (pair, pl.ds(r, BLK, stride=16)), first=True, final=False)

    def contiguous_branch(nblk, q_ref, k_ref, kp_ref, v_ref, vp_ref, seq_start, rows_of, final):
        for n in range(nblk):
            for pair in range(N_SLABS):
                ls = slice(pair * LANES, (pair + 1) * LANES)
                if n == 0:
                    kk = (kp_ref[:, ls], k_ref[0:BLK, ls])
                    vv = (vp_ref[:, ls], v_ref[0:BLK, ls])
                else:
                    kk = (k_ref[(n - 1) * BLK:n * BLK, ls], k_ref[n * BLK:(n + 1) * BLK, ls])
                    vv = (v_ref[(n - 1) * BLK:n * BLK, ls], v_ref[n * BLK:(n + 1) * BLK, ls])
                unit(q_ref[n * BLK:(n + 1) * BLK, ls], kk, vv, seq_start if n == 0 else False, (pair, rows_of(n)),
                     first=False, final=final, out_rows=slice(n * BLK, (n + 1) * BLK))

    @pl.when((ph >= PHASES_PER_BRANCH) & (ph < 2 * PHASES_PER_BRANCH))
    def _():
        for cl in range(CLASSES4):
            r = (ph - PHASES_PER_BRANCH) * CLASSES4 + cl
            contiguous_branch(ROWS4 // BLK, q4_ref.at[cl], k4_ref.at[cl], k4p_ref.at[cl], v4_ref.at[cl],
                              v4p_ref.at[cl], c == 0, lambda n: pl.ds(4 * BLK * n + r, BLK, stride=4), final=False)

    @pl.when(ph >= 2 * PHASES_PER_BRANCH)
    def _():
        i = ph - 2 * PHASES_PER_BRANCH
        contiguous_branch(TQ // BLK, q1_ref, k1_ref, k1p_ref, v1_ref, v1p_ref, (c == 0) & (i == 0),
                          lambda n: pl.ds(pl.multiple_of(i * TQ + n * BLK, BLK), BLK), final=True)


def _p_attn(q1, k1, v1, q4, k4, v4, q16, k16, v16):
    B, S, _ = q1.shape
    P = PHASES_PER_BRANCH
    assert S % CHUNK == 0 and TQ % BLK == 0
    d16 = pl.BlockSpec((None, CLASSES16, BLK, ATTN_W), lambda b, c, p: (b, _clip(p, P - 1), c, 0))
    d16p = pl.BlockSpec((None, CLASSES16, BLK, ATTN_W),
                        lambda b, c, p: (b, _clip(p, P - 1), jnp.maximum(c - 1, 0), 0))
    d4 = pl.BlockSpec((None, CLASSES4, ROWS4, ATTN_W), lambda b, c, p: (b, _clip(p - P, P - 1), c, 0))
    d4p = pl.BlockSpec((None, CLASSES4, BLK, ATTN_W),
                       lambda b, c, p: (b, _clip(p - P, P - 1), jnp.maximum(c * (ROWS4 // BLK) - 1, 0), 0))
    d1 = pl.BlockSpec((None, TQ, ATTN_W), lambda b, c, p: (b, c * P + _clip(p - 2 * P, P - 1), 0))
    d1p = pl.BlockSpec((None, BLK, ATTN_W),
                       lambda b, c, p: (b, jnp.maximum((c * P + _clip(p - 2 * P, P - 1)) * (TQ // BLK) - 1, 0), 0))
    return pl.pallas_call(
        _p_attn_kernel,
        grid=(B, S // CHUNK, 3 * P),
        in_specs=[d16, d16, d16p, d16, d16p, d4, d4, d4p, d4, d4p, d1, d1, d1p, d1, d1p],
        out_specs=d1,
        out_shape=jax.ShapeDtypeStruct((B, S, ATTN_W), BF16),
        scratch_shapes=[pltpu.VMEM((N_SLABS, CHUNK, LANES), F32)] * 3
        + [pltpu.VMEM((2, BLK, 2 * BLK), F32), pltpu.VMEM((2, 2 * BLK, 2 * BLK), F32),
           pltpu.VMEM((2, BLK, 4 * BLK), BF16), pltpu.VMEM((2, BLK, LANES), F32), pltpu.VMEM((2, BLK, LANES), F32)],
        compiler_params=_params(("arbitrary",) * 3),
        name="p_attn",
    )(q16, k16, k16, v16, v16, q4, k4, k4, v4, v4, q1, k1, k1, v1, v1)


def _p_memkv_kernel(mem_ref, g_ref, wk_ref, wv_ref, k_ref, v_ref, kb_ref, vb_ref):
    h = _rms(mem_ref[...], g_ref[...]).astype(BF16)
    k = _dot(h, wk_ref[...])
    v = _dot(h, wv_ref[...])
    kb_ref[...] = k.astype(BF16)
    vb_ref[...] = v.astype(BF16)
    for hd in range(N_XHEADS):
        k_ref[:, hd, :] = k[:, hd * XHEAD_DIM:(hd + 1) * XHEAD_DIM]
        v_ref[:, hd, :] = v[:, hd * XHEAD_DIM:(hd + 1) * XHEAD_DIM]


def _p_memkv(mem, g, wk, wv):
    B, M, _ = mem.shape
    blk = pl.BlockSpec((None, M, D_MODEL), lambda b: (b, 0, 0))
    oblk = pl.BlockSpec((None, M, N_XHEADS, XHEAD_DIM), lambda b: (b, 0, 0, 0))
    return pl.pallas_call(
        _p_memkv_kernel,
        grid=(B,),
        in_specs=[blk, _const_spec(g.shape), _const_spec(wk.shape), _const_spec(wv.shape)],
        out_specs=[oblk, oblk, blk, blk],
        out_shape=[jax.ShapeDtypeStruct((B, M, N_XHEADS, XHEAD_DIM), F32)] * 2
        + [jax.ShapeDtypeStruct((B, M, D_MODEL), BF16)] * 2,
        compiler_params=_params(("arbitrary",)),
        name="p_memkv",
    )(mem, g, wk, wv)


def _p_post_kernel(attn_ref, c_ref, x_ref, mk_ref, mv_ref, lng_ref, lnb_ref, goc_ref, goa_ref, wout_ref, gpm_ref,
                   gpre_ref, wq_ref, wo_ref, gpost_ref, o_ref):
    ma = _rms(attn_ref[...].astype(F32), goa_ref[...]).astype(BF16)
    mc = _rms(_conv_post(c_ref[...], lng_ref[...], lnb_ref[...]), goc_ref[...]).astype(BF16)
    y = _dot(ma, wout_ref[0:ATTN_W, :]) + _dot(mc, wout_ref[ATTN_W:D_MODEL, :])
    x1 = x_ref[...] + _rms(y, gpm_ref[...])
    h = _rms(x1, gpre_ref[...]).astype(BF16)
    qx = (_dot(h, wq_ref[...]) * (XHEAD_DIM ** -0.5)).astype(BF16)
    ox = _xattn_heads(qx, mk_ref, mv_ref)
    o_ref[...] = x1 + _rms(_dot(ox, wo_ref[...]), gpost_ref[...])


def _p_post(attn, c, x, mk, mv, lng, lnb, goc, goa, wout, gpm, gpre, wq, wo, gpost, tm):
    B, S, _ = x.shape
    tok = lambda c: pl.BlockSpec((None, tm, c), lambda b, j: (b, j, 0))
    mem = pl.BlockSpec((None, N_MEM, D_MODEL), lambda b, j: (b, 0, 0))
    consts = [lng, lnb, goc, goa, wout, gpm, gpre, wq, wo, gpost]
    return pl.pallas_call(
        _p_post_kernel,
        grid=(B, S // tm),
        in_specs=[tok(ATTN_W), tok(CONV_CH), tok(D_MODEL), mem, mem] + [_const_spec(a.shape) for a in consts],
        out_specs=tok(D_MODEL),
        out_shape=jax.ShapeDtypeStruct((B, S, D_MODEL), F32),
        compiler_params=_params(("arbitrary", "arbitrary")),
        name="p_post",
    )(attn, c, x, mk, mv, *consts)


FFN_PAD = 8


def _p_ffn_kernel(x_ref, gpre_ref, wup_ref, cw_ref, wdn_ref, gpost_ref, y_ref, st_ref, ext_ref, g_ref, *, tm):
    j = pl.program_id(1)

    @pl.when(j == 0)
    def _():
        ext_ref[0:FFN_PAD, :] = jnp.zeros((FFN_PAD, D_FF), F32)

    x = x_ref[...]
    h = _rms(x, gpre_ref[...]).astype(BF16)
    for c0 in range(0, D_FF, FF_CHUNK):
        cs = slice(c0, c0 + FF_CHUNK)
        ext_ref[FFN_PAD:FFN_PAD + tm, cs] = _dot(h, wup_ref[:, cs])
        lin = _dot(h, wup_ref[:, D_FF + c0:D_FF + c0 + FF_CHUNK])
        f = (cw_ref[0:1, cs] * ext_ref[FFN_PAD - 2:FFN_PAD - 2 + tm, cs]
             + cw_ref[1:2, cs] * ext_ref[FFN_PAD - 1:FFN_PAD - 1 + tm, cs]
             + cw_ref[2:3, cs] * ext_ref[FFN_PAD:FFN_PAD + tm, cs])
        g_ref[:, cs] = _ffn_act(f, lin)
    tail = ext_ref[tm:tm + FFN_PAD, :]
    ext_ref[0:FFN_PAD, :] = tail
    st_ref[...] = tail[FFN_PAD - (FFN_K - 1):FFN_PAD, :]
    y_ref[...] = x + _rms(_dot(g_ref[...], wdn_ref[...]), gpost_ref[...])


def _p_ffn(x, gpre, wup, cw, wdn, gpost, tm):
    B, S, _ = x.shape
    tok = pl.BlockSpec((None, tm, D_MODEL), lambda b, j: (b, j, 0))
    st = pl.BlockSpec((None, FFN_K - 1, D_FF), lambda b, j: (b, 0, 0))
    return pl.pallas_call(
        functools.partial(_p_ffn_kernel, tm=tm),
        grid=(B, S // tm),
        in_specs=[tok, _const_spec(gpre.shape), _const_spec(wup.shape), _const_spec(cw.shape),
                  _const_spec(wdn.shape), _const_spec(gpost.shape)],
        out_specs=[tok, st],
        out_shape=[jax.ShapeDtypeStruct((B, S, D_MODEL), F32), jax.ShapeDtypeStruct((B, FFN_K - 1, D_FF), F32)],
        scratch_shapes=[pltpu.VMEM((FFN_PAD + tm, D_FF), F32), pltpu.VMEM((tm, D_FF), BF16)],
        compiler_params=_params(("arbitrary", "arbitrary")),
        name="p_ffn",
    )(x, gpre, wup, cw, wdn, gpost)


def _tcol(t, c):
    return slice(t * c, (t + 1) * c)


def _single_step_call(kernel, ins, out_shapes, name, scratch=()):
    return pl.pallas_call(
        kernel,
        grid=(1,),
        in_specs=[_const_spec(a.shape) for a in ins],
        out_specs=[pl.BlockSpec(s.shape, lambda *_, nd=len(s.shape): (0,) * nd) for s in out_shapes],
        out_shape=out_shapes,
        scratch_shapes=list(scratch),
        compiler_params=_params(("arbitrary",)),
        name=name,
    )(*ins)


def _s_inproj_kernel(x_ref, g_ref, w_ref, q_ref, k_ref, v_ref, kt_ref, vt_ref, u_ref):
    bd = x_ref.shape[0]
    g = g_ref[...]
    h = jnp.concatenate([_rms(x_ref[:, t, :], g).astype(BF16) for t in range(T_NEW)], axis=0)
    q = _dot(h, w_ref[:, 0:ATTN_W]) * (HEAD_DIM ** -0.5)
    k = _dot(h, w_ref[:, ATTN_W:2 * ATTN_W])
    v = _dot(h, w_ref[:, 2 * ATTN_W:3 * ATTN_W])
    a = _dot(h, w_ref[:, 3 * ATTN_W:3 * ATTN_W + CONV_CH])
    gate = _dot(h, w_ref[:, 3 * ATTN_W + CONV_CH:3 * ATTN_W + 2 * CONV_CH])
    u = a * _sigmoid(gate)
    for t in range(T_NEW):
        rows = slice(t * bd, (t + 1) * bd)
        q_ref[:, _tcol(t, ATTN_W)] = q[rows]
        k_ref[:, _tcol(t, ATTN_W)] = k[rows]
        v_ref[:, _tcol(t, ATTN_W)] = v[rows]
        kt_ref[t] = k[rows].T
        vt_ref[t] = v[rows].T
        u_ref[:, _tcol(t, CONV_CH)] = u[rows]


def _s_inproj(x, g, w_in):
    bd = x.shape[0]
    flat = jax.ShapeDtypeStruct((bd, T_NEW * ATTN_W), F32)
    tr = jax.ShapeDtypeStruct((T_NEW, ATTN_W, bd), F32)
    return _single_step_call(_s_inproj_kernel, [x, g, w_in], [flat, flat, flat, tr, tr, flat], "s_inproj")


SB = 8
NEW_PAD = 8


def _sample_key_multiplicity():
    t = np.arange(T_NEW)[:, None]
    dist = MAX_WINDOW + t - np.arange(MAX_WINDOW)[None, :]
    mult = np.zeros((T_NEW, MAX_WINDOW), np.float32)
    for d in DILATIONS:
        mult += (dist % d == 0) & (dist // d >= 1) & (dist // d <= BLK)
    tn = np.arange(NEW_PAD)[None, :]
    new = np.where(tn == t, float(len(DILATIONS)), np.where(tn < t, 1.0, 0.0)).astype(np.float32)
    return np.repeat(mult, N_HEADS, axis=0), np.repeat(new, N_HEADS, axis=0)


def _s_attn_kernel(q_ref, kn_ref, vn_ref, kt_ref, vt_ref, mw_ref, mn_ref, o_ref):
    ii = pl.program_id(1)
    rows8 = lax.broadcasted_iota(jnp.int32, (SB, ATTN_W), 0)
    sel = rows8 == ii

    @pl.when(ii == 0)
    def _():
        o_ref[...] = jnp.zeros(o_ref.shape, F32)

    def pick(ref, t):
        return jnp.sum(jnp.where(sel, ref[:, _tcol(t, ATTN_W)], 0.0), axis=0, keepdims=True)

    head_of_lane = lax.broadcasted_iota(jnp.int32, (N_HEADS, ATTN_W), 1) // HEAD_DIM
    diag = head_of_lane == lax.broadcasted_iota(jnp.int32, (N_HEADS, ATTN_W), 0)
    qbd = jnp.concatenate([jnp.where(diag, pick(q_ref, t), 0.0) for t in range(T_NEW)], axis=0).astype(BF16)
    pad = jnp.zeros((NEW_PAD - T_NEW, ATTN_W), F32)
    k_new = jnp.concatenate([pick(kn_ref, t) for t in range(T_NEW)] + [pad], axis=0).astype(BF16)
    v_new = jnp.concatenate([pick(vn_ref, t) for t in range(T_NEW)] + [pad], axis=0).astype(BF16)

    kt = kt_ref[...].reshape(ATTN_W, MAX_WINDOW).astype(BF16)
    vt = vt_ref[...].reshape(ATTN_W, MAX_WINDOW).astype(BF16)
    mw = mw_ref[...]
    mn = mn_ref[...]
    s_w = jnp.where(mw > 0.0, _dot(qbd, kt), NEG)
    s_n = jnp.where(mn > 0.0, _dot_nt(qbd, k_new), NEG)
    m = jnp.maximum(jnp.max(s_w, axis=-1, keepdims=True), jnp.max(s_n, axis=-1, keepdims=True))
    p_w = jnp.exp(s_w - m) * mw
    p_n = jnp.exp(s_n - m) * mn
    l = jnp.sum(p_w, axis=-1, keepdims=True) + jnp.sum(p_n, axis=-1, keepdims=True)
    o_full = _dot_nt((p_w / l).astype(BF16), vt) + _dot((p_n / l).astype(BF16), v_new)
    for t in range(T_NEW):
        o = jnp.sum(jnp.where(diag, o_full[t * N_HEADS:(t + 1) * N_HEADS], 0.0), axis=0, keepdims=True)
        cur = o_ref[:, _tcol(t, ATTN_W)]
        o_ref[:, _tcol(t, ATTN_W)] = jnp.where(sel, o, cur)


def _s_attn(q, kn, vn, win_kt, win_vt):
    bd = q.shape[0]
    mw, mn = _sample_key_multiplicity()
    rowblk = pl.BlockSpec((SB, T_NEW * ATTN_W), lambda g, i: (g, 0))
    win = pl.BlockSpec((None, N_HEADS, HEAD_DIM, MAX_WINDOW), lambda g, i: (g * SB + i, 0, 0, 0))
    return pl.pallas_call(
        _s_attn_kernel,
        grid=(bd // SB, SB),
        in_specs=[rowblk, rowblk, rowblk, win, win, _const_spec(mw.shape), _const_spec(mn.shape)],
        out_specs=rowblk,
        out_shape=jax.ShapeDtypeStruct((bd, T_NEW * ATTN_W), F32),
        compiler_params=_params(("arbitrary", "arbitrary")),
        name="s_attn",
    )(q, kn, vn, win_kt, win_vt, jnp.asarray(mw), jnp.asarray(mn))


def _s_mix_kernel(attn_ref, u_ref, st_ref, x_ref, cw_ref, cb_ref, lng_ref, lnb_ref, goa_ref, goc_ref,
                  wout_ref, gpm_ref, gpre_ref, wq_ref, x1_ref, qx_ref, nst_ref):
    bd = x_ref.shape[0]
    nctx = CONV_K - 1

    def ctx(r):
        return st_ref[r] if r < nctx else u_ref[:, _tcol(r - nctx, CONV_CH)]

    mcs, mas = [], []
    for t in range(T_NEW):
        acc = jnp.zeros((bd, CONV_CH), F32)
        for j in range(CONV_K):
            acc = acc + cw_ref[j:j + 1, :] * ctx(t + j)
        c = _conv_post(acc + cb_ref[...], lng_ref[...], lnb_ref[...])
        mcs.append(_rms(c, goc_ref[...]).astype(BF16))
        mas.append(_rms(attn_ref[:, _tcol(t, ATTN_W)], goa_ref[...]).astype(BF16))
    y = (_dot(jnp.concatenate(mas, axis=0), wout_ref[0:ATTN_W, :])
         + _dot(jnp.concatenate(mcs, axis=0), wout_ref[ATTN_W:D_MODEL, :]))
    hs = []
    for t in range(T_NEW):
        x1 = x_ref[:, t, :] + _rms(y[t * bd:(t + 1) * bd], gpm_ref[...])
        x1_ref[:, _tcol(t, D_MODEL)] = x1
        hs.append(_rms(x1, gpre_ref[...]).astype(BF16))
    q = _dot(jnp.concatenate(hs, axis=0), wq_ref[...]) * (XHEAD_DIM ** -0.5)
    for t in range(T_NEW):
        qx_ref[:, _tcol(t, D_MODEL)] = q[t * bd:(t + 1) * bd]
    for r in range(nctx):
        nst_ref[r] = ctx(r + T_NEW)


def _s_mix(attn, u, state, x, cw, cb, lng, lnb, goa, goc, wout, gpm, gpre, wq):
    bd = x.shape[0]
    full = jax.ShapeDtypeStruct((bd, T_NEW * D_MODEL), F32)
    outs = [full, full, jax.ShapeDtypeStruct(state.shape, F32)]
    ins = [attn, u, state, x, cw, cb, lng, lnb, goa, goc, wout, gpm, gpre, wq]
    return _single_step_call(_s_mix_kernel, ins, outs, "s_mix")


XB = 4


def _s_xcore_kernel(q_ref, mk_ref, mv_ref, o_ref):
    ii = pl.program_id(1)
    rows8 = lax.broadcasted_iota(jnp.int32, (SB, D_MODEL), 0)

    @pl.when(ii == 0)
    def _():
        o_ref[...] = jnp.zeros(o_ref.shape, F32)

    nq = T_NEW * N_XHEADS
    qrow = lax.broadcasted_iota(jnp.int32, (nq, XHEAD_DIM), 0)
    srow = lax.broadcasted_iota(jnp.int32, (nq, N_MEM * N_XHEADS), 0)
    scol = lax.broadcasted_iota(jnp.int32, (nq, N_MEM * N_XHEADS), 1)
    same_head = srow % N_XHEADS == scol % N_XHEADS
    for xb in range(XB):
        sel = rows8 == ii * XB + xb
        q16 = jnp.zeros((nq, XHEAD_DIM), F32)
        for t in range(T_NEW):
            qt = jnp.sum(jnp.where(sel, q_ref[:, _tcol(t, D_MODEL)], 0.0), axis=0, keepdims=True)
            for h in range(N_XHEADS):
                q16 = jnp.where(qrow == t * N_XHEADS + h, qt[:, _tcol(h, XHEAD_DIM)], q16)
        k2 = mk_ref[xb].reshape(N_MEM * N_XHEADS, XHEAD_DIM).astype(BF16)
        v2 = mv_ref[xb].reshape(N_MEM * N_XHEADS, XHEAD_DIM).astype(BF16)
        s = jnp.where(same_head, _dot_nt(q16.astype(BF16), k2), NEG)
        m = jnp.max(s, axis=-1, keepdims=True)
        p = jnp.exp(s - m)
        l = jnp.sum(p, axis=-1, keepdims=True)
        o16 = _dot((p / l).astype(BF16), v2)
        for t in range(T_NEW):
            ot = jnp.concatenate([o16[t * N_XHEADS + h:t * N_XHEADS + h + 1, :] for h in range(N_XHEADS)], axis=1)
            cur = o_ref[:, _tcol(t, D_MODEL)]
            o_ref[:, _tcol(t, D_MODEL)] = jnp.where(sel, ot, cur)


def _s_xcore(q, mem_k, mem_v):
    bd = q.shape[0]
    assert SB % XB == 0
    rowblk = pl.BlockSpec((SB, T_NEW * D_MODEL), lambda g, i: (g, 0))
    mem = pl.BlockSpec((XB, N_MEM, N_XHEADS, XHEAD_DIM), lambda g, i: (g * (SB // XB) + i, 0, 0, 0))
    return pl.pallas_call(
        _s_xcore_kernel,
        grid=(bd // SB, SB // XB),
        in_specs=[rowblk, mem, mem],
        out_specs=rowblk,
        out_shape=jax.ShapeDtypeStruct((bd, T_NEW * D_MODEL), F32),
        compiler_params=_params(("arbitrary", "arbitrary")),
        name="s_xcore",
    )(q, mem_k, mem_v)


def _s_ffn_kernel(ox_ref, x1_ref, st_ref, wo_ref, gqx_ref, gpre_ref, wup_ref, cw_ref, wdn_ref, gpost_ref,
                  y_ref, nst_ref, g_ref):
    bd = x1_ref.shape[0]
    ox = jnp.concatenate([ox_ref[:, _tcol(t, D_MODEL)].astype(BF16) for t in range(T_NEW)], axis=0)
    yx = _dot(ox, wo_ref[...])
    x2 = [x1_ref[:, _tcol(t, D_MODEL)] + _rms(yx[t * bd:(t + 1) * bd], gqx_ref[...]) for t in range(T_NEW)]
    h = jnp.concatenate([_rms(x2[t], gpre_ref[...]).astype(BF16) for t in range(T_NEW)], axis=0)
    for c0 in range(0, D_FF, FF_CHUNK):
        cs = slice(c0, c0 + FF_CHUNK)
        act = _dot(h, wup_ref[:, cs])
        lin = _dot(h, wup_ref[:, D_FF + c0:D_FF + c0 + FF_CHUNK])
        a = [st_ref[:, r, cs] for r in range(FFN_K - 1)]
        a += [act[t * bd:(t + 1) * bd] for t in range(T_NEW)]
        for t in range(T_NEW):
            f = cw_ref[0:1, cs] * a[t] + cw_ref[1:2, cs] * a[t + 1] + cw_ref[2:3, cs] * a[t + 2]
            g_ref[t * bd:(t + 1) * bd, cs] = _ffn_act(f, lin[t * bd:(t + 1) * bd])
        for r in range(FFN_K - 1):
            nst_ref[:, r, cs] = a[T_NEW + r]
    y = _dot(g_ref[...], wdn_ref[...])
    for t in range(T_NEW):
        y_ref[:, t, :] = x2[t] + _rms(y[t * bd:(t + 1) * bd], gpost_ref[...])


def _s_ffn(ox, x1, state, wo, gqx, gpre, wup, cw, wdn, gpost):
    bd = x1.shape[0]
    outs = [jax.ShapeDtypeStruct((bd, T_NEW, D_MODEL), F32), jax.ShapeDtypeStruct(state.shape, F32)]
    return _single_step_call(_s_ffn_kernel, [ox, x1, state, wo, gqx, gpre, wup, cw, wdn, gpost], outs, "s_ffn",
                             scratch=[pltpu.VMEM((T_NEW * bd, D_FF), BF16)])


def kernel(x_prompt, x_sample, mem_prompt, cache_win_k, cache_win_v, state_conv, state_ffn_conv, cache_mem_k, cache_mem_v, g_pre_mix, w_in, conv_w, conv_b, ln_conv_g, ln_conv_b, g_out_attn, g_out_conv, w_out, g_post_mix, g_pre_x, g_mem, w_xq, w_mk, w_mv, w_xo, g_post_x, g_pre_ffn, w_up, ffn_conv_w, w_down, g_post_ffn):
    depth = w_in.shape[0]
    assert depth == 1
    B, S, _ = x_prompt.shape
    Bd, T, _ = x_sample.shape
    assert T == T_NEW and cache_win_k.shape[2] == MAX_WINDOW and S % CHUNK == 0 and Bd % SB == 0

    gpm, gpx, gm, gqx, gpf, gqf = (_row(a[0]) for a in (g_pre_mix, g_pre_x, g_mem, g_post_x, g_pre_ffn, g_post_ffn))
    gpo, goa, goc = _row(g_post_mix[0]), _row(g_out_attn[0]), _row(g_out_conv[0])
    cb, lng, lnb = _row(conv_b[0]), _row(ln_conv_g[0]), _row(ln_conv_b[0])
    cw, fcw = conv_w[0], ffn_conv_w[0]
    win, wout, wxq, wmk, wmv, wxo, wup, wdn = (
        a[0].astype(BF16) for a in (w_in, w_out, w_xq, w_mk, w_mv, w_xo, w_up, w_down))
    W = min(MAX_WINDOW, S)

    (q1, k1, v1, q4, k4, v4, q16, k16, v16, kt_p, vt_p, u_p) = _p_inproj(x_prompt, gpm, win, tm=512)
    c_p = _p_conv(u_p, cw, cb, tm=1024)
    attn = _p_attn(q1, k1, v1, q4, k4, v4, q16, k16, v16)
    mk_p, mv_p, mkb, mvb = _p_memkv(mem_prompt, gm, wmk, wmv)
    x2 = _p_post(attn, c_p, x_prompt, mkb, mvb, lng, lnb, goc, goa, wout, gpo, gpx, wxq, wxo, gqx, tm=512)
    y_p, ffn_p = _p_ffn(x2, gpf, wup, fcw, wdn, gqf, tm=512)
    conv_p = u_p[:, S - (CONV_K - 1):, :]
    to_win = lambda t: jnp.transpose(t.reshape(1, B, N_HEADS, HEAD_DIM, W), (0, 1, 4, 2, 3))

    qs, ks, vs, kt_s, vt_s, us = _s_inproj(x_sample, gpm, win)
    win_kt = jnp.transpose(cache_win_k[0], (0, 2, 3, 1))
    win_vt = jnp.transpose(cache_win_v[0], (0, 2, 3, 1))
    attn_s = _s_attn(qs, ks, vs, win_kt, win_vt)
    st_conv = jnp.transpose(state_conv[0], (1, 0, 2))
    x1s, qxs, conv_s = _s_mix(attn_s, us, st_conv, x_sample, cw, cb, lng, lnb, goa, goc, wout, gpo, gpx, wxq)
    oxs = _s_xcore(qxs, cache_mem_k[0], cache_mem_v[0])
    y_s, ffn_s = _s_ffn(oxs, x1s, state_ffn_conv[0], wxo, gqx, gpf, wup, fcw, wdn, gqf)
    to_rows = lambda t: jnp.transpose(t.reshape(1, T, N_HEADS, HEAD_DIM, Bd), (0, 4, 1, 2, 3))

    return (y_p, y_s, to_win(kt_p), to_win(vt_p), conv_p[None], ffn_p[None], mk_p[None], mv_p[None],
            to_rows(kt_s), to_rows(vt_s), jnp.transpose(conv_s, (1, 0, 2))[None], ffn_s[None])
```

```python
import functools
import math

import jax
import jax.numpy as jnp
import numpy as np
from jax import lax
from jax.experimental import pallas as pl
from jax.experimental.pallas import tpu as pltpu

F32 = jnp.float32
BF16 = jnp.bfloat16

EPS = 1e-6
D_MODEL = 1024
N_HEADS = 8
HEAD_DIM = 64
ATTN_W = N_HEADS * HEAD_DIM
CONV_CH = D_MODEL - ATTN_W
CONV_K = 31
BLK = 128
DILATIONS = (16, 4, 1)
MAX_WINDOW = 2048
N_MEM = 256
N_XHEADS = 4
XHEAD_DIM = D_MODEL // N_XHEADS
D_FF = 2816
FFN_K = 3
T_NEW = 4
NEG = -1e30
LANES = 128
N_SLABS = ATTN_W // LANES
FF_CHUNK = 256
VMEM_LIMIT = 56 * 1024 * 1024

CHUNK = 16 * BLK
PHASES_PER_BRANCH = 2
TQ = CHUNK // PHASES_PER_BRANCH
CLASSES16 = 16 // PHASES_PER_BRANCH
CLASSES4 = 4 // PHASES_PER_BRANCH
ROWS4 = CHUNK // 4


def _rms(x, g):
    return x * lax.rsqrt(jnp.mean(x * x, axis=-1, keepdims=True) + EPS) * g


def _dot(a, b):
    return jnp.dot(a, b, preferred_element_type=F32)


def _dot_nt(a, b):
    return lax.dot_general(a, b, (((1,), (1,)), ((), ())), preferred_element_type=F32)


def _sigmoid(x):
    return 1.0 / (1.0 + jnp.exp(-x))


def _gelu_tanh(x):
    c = math.sqrt(2.0 / math.pi)
    return x * (0.5 * (1.0 + jnp.tanh(c * (x + 0.044715 * (x * x * x)))))


def _conv_post(c, lng, lnb):
    mu = jnp.mean(c, axis=-1, keepdims=True)
    d = c - mu
    var = jnp.mean(d * d, axis=-1, keepdims=True)
    y = d * lax.rsqrt(var + EPS) * lng + lnb
    return y * _sigmoid(y)


def _ffn_act(f, lin):
    return (_gelu_tanh(f) * lin).astype(BF16)


def _xattn_heads(qx, mk_ref, mv_ref):
    outs = []
    for h in range(N_XHEADS):
        hs = slice(h * XHEAD_DIM, (h + 1) * XHEAD_DIM)
        s = _dot_nt(qx[:, hs], mk_ref[:, hs])
        m = jnp.max(s, axis=-1, keepdims=True)
        p = jnp.exp(s - m)
        l = jnp.sum(p, axis=-1, keepdims=True)
        outs.append(_dot((p / l).astype(BF16), mv_ref[:, hs]).astype(BF16))
    return jnp.concatenate(outs, axis=-1)


def _const_spec(shape):
    nd = len(shape)
    return pl.BlockSpec(shape, lambda *_: (0,) * nd, pipeline_mode=pl.Buffered(1))


def _params(sem):
    return pltpu.CompilerParams(dimension_semantics=sem, vmem_limit_bytes=VMEM_LIMIT)


def _row(v):
    return v.reshape(1, -1).astype(F32)


def _clip(v, hi):
    return jnp.minimum(jnp.maximum(v, 0), hi)


HALO = 32
CONV_ROWS = 64


def _p_inproj_kernel(x_ref, g_ref, w_ref, cw_ref, cb_ref,
                     q1_ref, k1_ref, v1_ref, q4_ref, k4_ref, v4_ref, q16_ref, k16_ref, v16_ref,
                     kt_ref, vt_ref, c_ref, cst_ref, z_ref, z4_ref_all, ext_ref, *, tm):
    j = pl.program_id(1)

    @pl.when(j == 0)
    def _():
        ext_ref[:, 0:HALO, :] = jnp.zeros((N_SLABS, HALO, LANES), F32)

    @pl.when(j > 0)
    def _():
        ext_ref[:, 0:HALO, :] = ext_ref[:, tm:tm + HALO, :]

    h = _rms(x_ref[...], g_ref[...]).astype(BF16)
    a = _dot(h, w_ref[:, 3 * ATTN_W:3 * ATTN_W + CONV_CH])
    gate = _dot(h, w_ref[:, 3 * ATTN_W + CONV_CH:3 * ATTN_W + 2 * CONV_CH])
    u = a * _sigmoid(gate)
    cst_ref[...] = u[tm - HALO:tm, :]
    for s in range(N_SLABS):
        ext_ref[s, HALO:HALO + tm, :] = u[:, s * LANES:(s + 1) * LANES]
    off = HALO - (CONV_K - 1)
    for s in range(N_SLABS):
        ls = slice(s * LANES, (s + 1) * LANES)
        for r0 in range(0, tm, CONV_ROWS):
            acc = jnp.zeros((CONV_ROWS, LANES), F32)
            for t in range(CONV_K):
                acc = acc + cw_ref[t:t + 1, ls] * ext_ref[s, r0 + off + t:r0 + off + t + CONV_ROWS, :]
            c_ref[r0:r0 + CONV_ROWS, ls] = acc + cb_ref[:, ls]

    def project(which, scale, tok_ref, d4_ref, d16_ref, t_ref):
        z = _dot(h, w_ref[:, which * ATTN_W:(which + 1) * ATTN_W])
        if scale != 1.0:
            z = z * scale
        tok_ref[...] = z.astype(BF16)
        if t_ref is not None:
            t_ref[...] = z.T
        zs_ref, z4_ref = z_ref.at[which], z4_ref_all.at[which]
        for s in range(N_SLABS):
            ls = slice(s * LANES, (s + 1) * LANES)
            zs_ref[s] = z[:, ls]
            for r in range(4):
                z4 = zs_ref[s, pl.ds(r, tm // 4, stride=4), :]
                d4_ref[r, :, ls] = z4.astype(BF16)
                z4_ref[s, r * (tm // 4):(r + 1) * (tm // 4), :] = z4
            for r in range(16):
                z16 = z4_ref[s, pl.ds((r % 4) * (tm // 4) + r // 4, tm // 16, stride=4), :]
                d16_ref[r, :, ls] = z16.astype(BF16)

    project(0, HEAD_DIM ** -0.5 * math.log2(math.e), q1_ref, q4_ref, q16_ref, None)
    project(1, 1.0, k1_ref, k4_ref, k16_ref, kt_ref)
    project(2, 1.0, v1_ref, v4_ref, v16_ref, vt_ref)


def _p_inproj(x, g, w_in, cw, cb, tm):
    B, S, _ = x.shape
    W = min(MAX_WINDOW, S)
    nt = S // tm
    first = (S - W) // tm
    tok = lambda c: pl.BlockSpec((None, tm, c), lambda b, j: (b, j, 0))
    cls = lambda d: pl.BlockSpec((None, d, tm // d, ATTN_W), lambda b, j: (b, 0, j, 0))
    win = pl.BlockSpec((None, ATTN_W, tm), lambda b, j: (b, 0, jnp.maximum(j - first, 0)))
    cst = pl.BlockSpec((None, HALO, CONV_CH), lambda b, j: (b, 0, 0))
    bf = lambda *shape: jax.ShapeDtypeStruct(shape, BF16)
    consts = [g, w_in, cw, cb]
    return pl.pallas_call(
        functools.partial(_p_inproj_kernel, tm=tm),
        grid=(B, nt),
        in_specs=[tok(D_MODEL)] + [_const_spec(a.shape) for a in consts],
        out_specs=[tok(ATTN_W)] * 3 + [cls(4)] * 3 + [cls(16)] * 3 + [win, win, tok(CONV_CH), cst],
        out_shape=[bf(B, S, ATTN_W)] * 3 + [bf(B, 4, S // 4, ATTN_W)] * 3 + [bf(B, 16, S // 16, ATTN_W)] * 3
        + [jax.ShapeDtypeStruct((B, ATTN_W, W), F32)] * 2
        + [jax.ShapeDtypeStruct((B, S, CONV_CH), F32), jax.ShapeDtypeStruct((B, HALO, CONV_CH), F32)],
        scratch_shapes=[pltpu.VMEM((3, N_SLABS, tm, LANES), F32)] * 2
        + [pltpu.VMEM((N_SLABS, HALO + tm, LANES), F32)],
        compiler_params=_params(("arbitrary", "arbitrary")),
        name="p_inproj",
    )(x, *consts)


SM_ROWS = 32


def _pair_scores(qp, k_parts, v_parts, bias_ref, s_ref, pb_ref, mb_ref, lb_ref):
    lo = lax.broadcasted_iota(jnp.int32, (BLK, LANES), 1) < HEAD_DIM
    los = lax.broadcasted_iota(jnp.int32, (SM_ROWS, LANES), 1) < HEAD_DIM
    zq = jnp.zeros_like(qp)
    q2 = jnp.concatenate([jnp.where(lo, qp, zq), jnp.where(lo, zq, qp)], axis=0)
    s_ref[...] = _dot_nt(q2, jnp.concatenate(k_parts, axis=0))
    for r0 in range(0, BLK, SM_ROWS):
        rows = slice(r0, r0 + SM_ROWS)
        bias = bias_ref[rows, :]
        stats = []
        for hh in range(2):
            sc = s_ref[hh * BLK + r0:hh * BLK + r0 + SM_ROWS, :] + bias
            m = jnp.max(sc, axis=-1, keepdims=True)
            p = jnp.exp2(sc - m)
            stats.append((m, jnp.sum(p, axis=-1, keepdims=True)))
            pb_ref[rows, hh * 2 * BLK:(hh + 1) * 2 * BLK] = p.astype(BF16)
        mb_ref[rows, :] = jnp.where(los, stats[0][0], stats[1][0])
        lb_ref[rows, :] = jnp.where(los, stats[0][1], stats[1][1])
    pv = []
    for hh in range(2):
        pv.append(sum(_dot(pb_ref[:, (2 * hh + part) * BLK:(2 * hh + part + 1) * BLK], vp)
                      for part, vp in enumerate(v_parts)))
    return jnp.where(lo, pv[0], pv[1])


def _p_attn_kernel(q16_ref, k16_ref, k16p_ref, v16_ref, v16p_ref,
                   q4_ref, k4_ref, k4p_ref, v4_ref, v4p_ref,
                   q1_ref, k1_ref, k1p_ref, v1_ref, v1p_ref,
                   o_ref, m_ref, l_ref, a_ref, bias_ref, s_ref, pb_ref, mb_ref, lb_ref):
    c = pl.program_id(1)
    ph = pl.program_id(2)

    @pl.when((c == 0) & (ph == 0))
    def _():
        row = lax.broadcasted_iota(jnp.int32, (BLK, 2 * BLK), 0)
        col = lax.broadcasted_iota(jnp.int32, (BLK, 2 * BLK), 1)
        band = (col >= row) & (col <= row + BLK)
        bias_ref[0] = jnp.where(band, 0.0, NEG)
        bias_ref[1] = jnp.where(band & (col >= BLK), 0.0, NEG)

    def unit(qp, kk, vv, seq_start, state_rows, first, final, out_rows=None):
        pair, rows = state_rows
        bias = bias_ref.at[jnp.where(seq_start, 1, 0)] if not isinstance(seq_start, bool) else bias_ref.at[int(seq_start)]
        buf = pair % 2
        pv = _pair_scores(qp, kk, vv, bias, s_ref.at[buf], pb_ref.at[buf], mb_ref.at[buf], lb_ref.at[buf])
        m_b = mb_ref[buf]
        l_b = lb_ref[buf]
        if first:
            m_new, l_new, a_new = m_b, l_b, pv
        else:
            m0 = m_ref[pair, rows, :]
            m_new = jnp.maximum(m0, m_b)
            a0 = jnp.exp2(m0 - m_new)
            a1 = jnp.exp2(m_b - m_new)
            l_new = a0 * l_ref[pair, rows, :] + a1 * l_b
            a_new = a0 * a_ref[pair, rows, :] + a1 * pv
        if final:
            o_ref[out_rows, pair * LANES:(pair + 1) * LANES] = (a_new / l_new).astype(o_ref.dtype)
        else:
            m_ref[pair, rows, :] = m_new
            l_ref[pair, rows, :] = l_new
            a_ref[pair, rows, :] = a_new

    @pl.when(ph < PHASES_PER_BRANCH)
    def _():
        for cl in range(CLASSES16):
            r = ph * CLASSES16 + cl
            for pair in range(N_SLABS):
                ls = slice(pair * LANES, (pair + 1) * LANES)
                kk = (k16p_ref[cl, :, ls], k16_ref[cl, :, ls])
                vv = (v16p_ref[cl, :, ls], v16_ref[cl, :, ls])
                unit(q16_ref[cl, :, ls], kk, vv, c == 0, (pair, pl.ds(r, BLK, stride=16)), first=True, final=False)

    def contiguous_branch(nblk, q_ref, k_ref, kp_ref, v_ref, vp_ref, seq_start, rows_of, final):
        for n in range(nblk):
            for pair in range(N_SLABS):
                ls = slice(pair * LANES, (pair + 1) * LANES)
                if n == 0:
                    kk = (kp_ref[:, ls], k_ref[0:BLK, ls])
                    vv = (vp_ref[:, ls], v_ref[0:BLK, ls])
                else:
                    kk = (k_ref[(n - 1) * BLK:n * BLK, ls], k_ref[n * BLK:(n + 1) * BLK, ls])
                    vv = (v_ref[(n - 1) * BLK:n * BLK, ls], v_ref[n * BLK:(n + 1) * BLK, ls])
                unit(q_ref[n * BLK:(n + 1) * BLK, ls], kk, vv, seq_start if n == 0 else False, (pair, rows_of(n)),
                     first=False, final=final, out_rows=slice(n * BLK, (n + 1) * BLK))

    @pl.when((ph >= PHASES_PER_BRANCH) & (ph < 2 * PHASES_PER_BRANCH))
    def _():
        for cl in range(CLASSES4):
            r = (ph - PHASES_PER_BRANCH) * CLASSES4 + cl
            contiguous_branch(ROWS4 // BLK, q4_ref.at[cl], k4_ref.at[cl], k4p_ref.at[cl], v4_ref.at[cl],
                              v4p_ref.at[cl], c == 0, lambda n: pl.ds(4 * BLK * n + r, BLK, stride=4), final=False)

    @pl.when(ph >= 2 * PHASES_PER_BRANCH)
    def _():
        i = ph - 2 * PHASES_PER_BRANCH
        contiguous_branch(TQ // BLK, q1_ref, k1_ref, k1p_ref, v1_ref, v1p_ref, (c == 0) & (i == 0),
                          lambda n: pl.ds(pl.multiple_of(i * TQ + n * BLK, BLK), BLK), final=True)


def _p_attn(q1, k1, v1, q4, k4, v4, q16, k16, v16):
    B, S, _ = q1.shape
    P = PHASES_PER_BRANCH
    assert S % CHUNK == 0 and TQ % BLK == 0
    d16 = pl.BlockSpec((None, CLASSES16, BLK, ATTN_W), lambda b, c, p: (b, _clip(p, P - 1), c, 0))
    d16p = pl.BlockSpec((None, CLASSES16, BLK, ATTN_W),
                        lambda b, c, p: (b, _clip(p, P - 1), jnp.maximum(c - 1, 0), 0))
    d4 = pl.BlockSpec((None, CLASSES4, ROWS4, ATTN_W), lambda b, c, p: (b, _clip(p - P, P - 1), c, 0))
    d4p = pl.BlockSpec((None, CLASSES4, BLK, ATTN_W),
                       lambda b, c, p: (b, _clip(p - P, P - 1), jnp.maximum(c * (ROWS4 // BLK) - 1, 0), 0))
    d1 = pl.BlockSpec((None, TQ, ATTN_W), lambda b, c, p: (b, c * P + _clip(p - 2 * P, P - 1), 0))
    d1p = pl.BlockSpec((None, BLK, ATTN_W),
                       lambda b, c, p: (b, jnp.maximum((c * P + _clip(p - 2 * P, P - 1)) * (TQ // BLK) - 1, 0), 0))
    return pl.pallas_call(
        _p_attn_kernel,
        grid=(B, S // CHUNK, 3 * P),
        in_specs=[d16, d16, d16p, d16, d16p, d4, d4, d4p, d4, d4p, d1, d1, d1p, d1, d1p],
        out_specs=d1,
        out_shape=jax.ShapeDtypeStruct((B, S, ATTN_W), BF16),
        scratch_shapes=[pltpu.VMEM((N_SLABS, CHUNK, LANES), F32)] * 3
        + [pltpu.VMEM((2, BLK, 2 * BLK), F32), pltpu.VMEM((2, 2 * BLK, 2 * BLK), F32),
           pltpu.VMEM((2, BLK, 4 * BLK), BF16), pltpu.VMEM((2, BLK, LANES), F32), pltpu.VMEM((2, BLK, LANES), F32)],
        compiler_params=_params(("arbitrary",) * 3),
        name="p_attn",
    )(q16, k16, k16, v16, v16, q4, k4, k4, v4, v4, q1, k1, k1, v1, v1)


def _p_memkv_kernel(mem_ref, g_ref, wk_ref, wv_ref, k_ref, v_ref, kb_ref, vb_ref):
    h = _rms(mem_ref[...], g_ref[...]).astype(BF16)
    k = _dot(h, wk_ref[...])
    v = _dot(h, wv_ref[...])
    kb_ref[...] = k.astype(BF16)
    vb_ref[...] = v.astype(BF16)
    for hd in range(N_XHEADS):
        k_ref[:, hd, :] = k[:, hd * XHEAD_DIM:(hd + 1) * XHEAD_DIM]
        v_ref[:, hd, :] = v[:, hd * XHEAD_DIM:(hd + 1) * XHEAD_DIM]


def _p_memkv(mem, g, wk, wv):
    B, M, _ = mem.shape
    blk = pl.BlockSpec((None, M, D_MODEL), lambda b: (b, 0, 0))
    oblk = pl.BlockSpec((None, M, N_XHEADS, XHEAD_DIM), lambda b: (b, 0, 0, 0))
    return pl.pallas_call(
        _p_memkv_kernel,
        grid=(B,),
        in_specs=[blk, _const_spec(g.shape), _const_spec(wk.shape), _const_spec(wv.shape)],
        out_specs=[oblk, oblk, blk, blk],
        out_shape=[jax.ShapeDtypeStruct((B, M, N_XHEADS, XHEAD_DIM), F32)] * 2
        + [jax.ShapeDtypeStruct((B, M, D_MODEL), BF16)] * 2,
        compiler_params=_params(("arbitrary",)),
        name="p_memkv",
    )(mem, g, wk, wv)


def _p_post_kernel(attn_ref, c_ref, x_ref, mk_ref, mv_ref, lng_ref, lnb_ref, goc_ref, goa_ref, wout_ref, gpm_ref,
                   gpre_ref, wq_ref, wo_ref, gpost_ref, o_ref):
    ma = _rms(attn_ref[...].astype(F32), goa_ref[...]).astype(BF16)
    mc = _rms(_conv_post(c_ref[...], lng_ref[...], lnb_ref[...]), goc_ref[...]).astype(BF16)
    y = _dot(ma, wout_ref[0:ATTN_W, :]) + _dot(mc, wout_ref[ATTN_W:D_MODEL, :])
    x1 = x_ref[...] + _rms(y, gpm_ref[...])
    h = _rms(x1, gpre_ref[...]).astype(BF16)
    qx = (_dot(h, wq_ref[...]) * (XHEAD_DIM ** -0.5)).astype(BF16)
    ox = _xattn_heads(qx, mk_ref, mv_ref)
    o_ref[...] = x1 + _rms(_dot(ox, wo_ref[...]), gpost_ref[...])


def _p_post(attn, c, x, mk, mv, lng, lnb, goc, goa, wout, gpm, gpre, wq, wo, gpost, tm):
    B, S, _ = x.shape
    tok = lambda c: pl.BlockSpec((None, tm, c), lambda b, j: (b, j, 0))
    mem = pl.BlockSpec((None, N_MEM, D_MODEL), lambda b, j: (b, 0, 0))
    consts = [lng, lnb, goc, goa, wout, gpm, gpre, wq, wo, gpost]
    return pl.pallas_call(
        _p_post_kernel,
        grid=(B, S // tm),
        in_specs=[tok(ATTN_W), tok(CONV_CH), tok(D_MODEL), mem, mem] + [_const_spec(a.shape) for a in consts],
        out_specs=tok(D_MODEL),
        out_shape=jax.ShapeDtypeStruct((B, S, D_MODEL), F32),
        compiler_params=_params(("arbitrary", "arbitrary")),
        name="p_post",
    )(attn, c, x, mk, mv, *consts)


FFN_PAD = 8


def _p_ffn_kernel(x_ref, gpre_ref, wup_ref, cw_ref, wdn_ref, gpost_ref, y_ref, st_ref, ext_ref, g_ref, *, tm):
    j = pl.program_id(1)

    @pl.when(j == 0)
    def _():
        ext_ref[0:FFN_PAD, :] = jnp.zeros((FFN_PAD, D_FF), F32)

    x = x_ref[...]
    h = _rms(x, gpre_ref[...]).astype(BF16)
    for c0 in range(0, D_FF, FF_CHUNK):
        cs = slice(c0, c0 + FF_CHUNK)
        ext_ref[FFN_PAD:FFN_PAD + tm, cs] = _dot(h, wup_ref[:, cs])
        lin = _dot(h, wup_ref[:, D_FF + c0:D_FF + c0 + FF_CHUNK])
        f = (cw_ref[0:1, cs] * ext_ref[FFN_PAD - 2:FFN_PAD - 2 + tm, cs]
             + cw_ref[1:2, cs] * ext_ref[FFN_PAD - 1:FFN_PAD - 1 + tm, cs]
             + cw_ref[2:3, cs] * ext_ref[FFN_PAD:FFN_PAD + tm, cs])
        g_ref[:, cs] = _ffn_act(f, lin)
    tail = ext_ref[tm:tm + FFN_PAD, :]
    ext_ref[0:FFN_PAD, :] = tail
    st_ref[...] = tail[FFN_PAD - (FFN_K - 1):FFN_PAD, :]
    y_ref[...] = x + _rms(_dot(g_ref[...], wdn_ref[...]), gpost_ref[...])


def _p_ffn(x, gpre, wup, cw, wdn, gpost, tm):
    B, S, _ = x.shape
    tok = pl.BlockSpec((None, tm, D_MODEL), lambda b, j: (b, j, 0))
    st = pl.BlockSpec((None, FFN_K - 1, D_FF), lambda b, j: (b, 0, 0))
    return pl.pallas_call(
        functools.partial(_p_ffn_kernel, tm=tm),
        grid=(B, S // tm),
        in_specs=[tok, _const_spec(gpre.shape), _const_spec(wup.shape), _const_spec(cw.shape),
                  _const_spec(wdn.shape), _const_spec(gpost.shape)],
        out_specs=[tok, st],
        out_shape=[jax.ShapeDtypeStruct((B, S, D_MODEL), F32), jax.ShapeDtypeStruct((B, FFN_K - 1, D_FF), F32)],
        scratch_shapes=[pltpu.VMEM((FFN_PAD + tm, D_FF), F32), pltpu.VMEM((tm, D_FF), BF16)],
        compiler_params=_params(("arbitrary", "arbitrary")),
        name="p_ffn",
    )(x, gpre, wup, cw, wdn, gpost)


def _tcol(t, c):
    return slice(t * c, (t + 1) * c)


def _single_step_call(kernel, ins, out_shapes, name, scratch=()):
    return pl.pallas_call(
        kernel,
        grid=(1,),
        in_specs=[_const_spec(a.shape) for a in ins],
        out_specs=[pl.BlockSpec(s.shape, lambda *_, nd=len(s.shape): (0,) * nd) for s in out_shapes],
        out_shape=out_shapes,
        scratch_shapes=list(scratch),
        compiler_params=_params(("arbitrary",)),
        name=name,
    )(*ins)


def _s_inproj_kernel(x_ref, g_ref, w_ref, q_ref, k_ref, v_ref, kt_ref, vt_ref, u_ref):
    bd = x_ref.shape[0]
    g = g_ref[...]
    h = jnp.concatenate([_rms(x_ref[:, t, :], g).astype(BF16) for t in range(T_NEW)], axis=0)
    q = _dot(h, w_ref[:, 0:ATTN_W]) * (HEAD_DIM ** -0.5)
    k = _dot(h, w_ref[:, ATTN_W:2 * ATTN_W])
    v = _dot(h, w_ref[:, 2 * ATTN_W:3 * ATTN_W])
    a = _dot(h, w_ref[:, 3 * ATTN_W:3 * ATTN_W + CONV_CH])
    gate = _dot(h, w_ref[:, 3 * ATTN_W + CONV_CH:3 * ATTN_W + 2 * CONV_CH])
    u = a * _sigmoid(gate)
    for t in range(T_NEW):
        rows = slice(t * bd, (t + 1) * bd)
        q_ref[:, _tcol(t, ATTN_W)] = q[rows]
        k_ref[:, _tcol(t, ATTN_W)] = k[rows]
        v_ref[:, _tcol(t, ATTN_W)] = v[rows]
        kt_ref[t] = k[rows].T
        vt_ref[t] = v[rows].T
        u_ref[:, _tcol(t, CONV_CH)] = u[rows]


def _s_inproj(x, g, w_in):
    bd = x.shape[0]
    flat = jax.ShapeDtypeStruct((bd, T_NEW * ATTN_W), F32)
    tr = jax.ShapeDtypeStruct((T_NEW, ATTN_W, bd), F32)
    return _single_step_call(_s_inproj_kernel, [x, g, w_in], [flat, flat, flat, tr, tr, flat], "s_inproj")


SB = 8
NEW_PAD = 8


def _sample_key_multiplicity():
    t = np.arange(T_NEW)[:, None]
    dist = MAX_WINDOW + t - np.arange(MAX_WINDOW)[None, :]
    mult = np.zeros((T_NEW, MAX_WINDOW), np.float32)
    for d in DILATIONS:
        mult += (dist % d == 0) & (dist // d >= 1) & (dist // d <= BLK)
    tn = np.arange(NEW_PAD)[None, :]
    new = np.where(tn == t, float(len(DILATIONS)), np.where(tn < t, 1.0, 0.0)).astype(np.float32)
    return np.repeat(mult, N_HEADS, axis=0), np.repeat(new, N_HEADS, axis=0)


def _s_attn_kernel(q_ref, kn_ref, vn_ref, kt_ref, vt_ref, mw_ref, mn_ref, o_ref):
    ii = pl.program_id(1)
    rows8 = lax.broadcasted_iota(jnp.int32, (SB, ATTN_W), 0)
    sel = rows8 == ii

    @pl.when(ii == 0)
    def _():
        o_ref[...] = jnp.zeros(o_ref.shape, F32)

    def pick(ref, t):
        return jnp.sum(jnp.where(sel, ref[:, _tcol(t, ATTN_W)], 0.0), axis=0, keepdims=True)

    head_of_lane = lax.broadcasted_iota(jnp.int32, (N_HEADS, ATTN_W), 1) // HEAD_DIM
    diag = head_of_lane == lax.broadcasted_iota(jnp.int32, (N_HEADS, ATTN_W), 0)
    qbd = jnp.concatenate([jnp.where(diag, pick(q_ref, t), 0.0) for t in range(T_NEW)], axis=0).astype(BF16)
    pad = jnp.zeros((NEW_PAD - T_NEW, ATTN_W), F32)
    k_new = jnp.concatenate([pick(kn_ref, t) for t in range(T_NEW)] + [pad], axis=0).astype(BF16)
    v_new = jnp.concatenate([pick(vn_ref, t) for t in range(T_NEW)] + [pad], axis=0).astype(BF16)

    kt = kt_ref[...].reshape(ATTN_W, MAX_WINDOW).astype(BF16)
    vt = vt_ref[...].reshape(ATTN_W, MAX_WINDOW).astype(BF16)
    mw = mw_ref[...]
    mn = mn_ref[...]
    s_w = jnp.where(mw > 0.0, _dot(qbd, kt), NEG)
    s_n = jnp.where(mn > 0.0, _dot_nt(qbd, k_new), NEG)
    m = jnp.maximum(jnp.max(s_w, axis=-1, keepdims=True), jnp.max(s_n, axis=-1, keepdims=True))
    p_w = jnp.exp(s_w - m) * mw
    p_n = jnp.exp(s_n - m) * mn
    l = jnp.sum(p_w, axis=-1, keepdims=True) + jnp.sum(p_n, axis=-1, keepdims=True)
    o_full = _dot_nt((p_w / l).astype(BF16), vt) + _dot((p_n / l).astype(BF16), v_new)
    for t in range(T_NEW):
        o = jnp.sum(jnp.where(diag, o_full[t * N_HEADS:(t + 1) * N_HEADS], 0.0), axis=0, keepdims=True)
        cur = o_ref[:, _tcol(t, ATTN_W)]
        o_ref[:, _tcol(t, ATTN_W)] = jnp.where(sel, o, cur)


def _s_attn(q, kn, vn, win_kt, win_vt):
    bd = q.shape[0]
    mw, mn = _sample_key_multiplicity()
    rowblk = pl.BlockSpec((SB, T_NEW * ATTN_W), lambda g, i: (g, 0))
    win = pl.BlockSpec((None, N_HEADS, HEAD_DIM, MAX_WINDOW), lambda g, i: (g * SB + i, 0, 0, 0))
    return pl.pallas_call(
        _s_attn_kernel,
        grid=(bd // SB, SB),
        in_specs=[rowblk, rowblk, rowblk, win, win, _const_spec(mw.shape), _const_spec(mn.shape)],
        out_specs=rowblk,
        out_shape=jax.ShapeDtypeStruct((bd, T_NEW * ATTN_W), F32),
        compiler_params=_params(("arbitrary", "arbitrary")),
        name="s_attn",
    )(q, kn, vn, win_kt, win_vt, jnp.asarray(mw), jnp.asarray(mn))


def _s_mix_kernel(attn_ref, u_ref, st_ref, x_ref, cw_ref, cb_ref, lng_ref, lnb_ref, goa_ref, goc_ref,
                  wout_ref, gpm_ref, gpre_ref, wq_ref, x1_ref, qx_ref, nst_ref):
    bd = x_ref.shape[0]
    nctx = CONV_K - 1

    def ctx(r):
        return st_ref[r] if r < nctx else u_ref[:, _tcol(r - nctx, CONV_CH)]

    mcs, mas = [], []
    for t in range(T_NEW):
        acc = jnp.zeros((bd, CONV_CH), F32)
        for j in range(CONV_K):
            acc = acc + cw_ref[j:j + 1, :] * ctx(t + j)
        c = _conv_post(acc + cb_ref[...], lng_ref[...], lnb_ref[...])
        mcs.append(_rms(c, goc_ref[...]).astype(BF16))
        mas.append(_rms(attn_ref[:, _tcol(t, ATTN_W)], goa_ref[...]).astype(BF16))
    y = (_dot(jnp.concatenate(mas, axis=0), wout_ref[0:ATTN_W, :])
         + _dot(jnp.concatenate(mcs, axis=0), wout_ref[ATTN_W:D_MODEL, :]))
    hs = []
    for t in range(T_NEW):
        x1 = x_ref[:, t, :] + _rms(y[t * bd:(t + 1) * bd], gpm_ref[...])
        x1_ref[:, _tcol(t, D_MODEL)] = x1
        hs.append(_rms(x1, gpre_ref[...]).astype(BF16))
    q = _dot(jnp.concatenate(hs, axis=0), wq_ref[...]) * (XHEAD_DIM ** -0.5)
    for t in range(T_NEW):
        qx_ref[:, _tcol(t, D_MODEL)] = q[t * bd:(t + 1) * bd]
    for r in range(nctx):
        nst_ref[r] = ctx(r + T_NEW)


def _s_mix(attn, u, state, x, cw, cb, lng, lnb, goa, goc, wout, gpm, gpre, wq):
    bd = x.shape[0]
    full = jax.ShapeDtypeStruct((bd, T_NEW * D_MODEL), F32)
    outs = [full, full, jax.ShapeDtypeStruct(state.shape, F32)]
    ins = [attn, u, state, x, cw, cb, lng, lnb, goa, goc, wout, gpm, gpre, wq]
    return _single_step_call(_s_mix_kernel, ins, outs, "s_mix")


XB = 4


def _s_xcore_kernel(q_ref, mk_ref, mv_ref, o_ref):
    ii = pl.program_id(1)
    rows8 = lax.broadcasted_iota(jnp.int32, (SB, D_MODEL), 0)

    @pl.when(ii == 0)
    def _():
        o_ref[...] = jnp.zeros(o_ref.shape, F32)

    nq = T_NEW * N_XHEADS
    qrow = lax.broadcasted_iota(jnp.int32, (nq, XHEAD_DIM), 0)
    srow = lax.broadcasted_iota(jnp.int32, (nq, N_MEM * N_XHEADS), 0)
    scol = lax.broadcasted_iota(jnp.int32, (nq, N_MEM * N_XHEADS), 1)
    same_head = srow % N_XHEADS == scol % N_XHEADS
    for xb in range(XB):
        sel = rows8 == ii * XB + xb
        q16 = jnp.zeros((nq, XHEAD_DIM), F32)
        for t in range(T_NEW):
            qt = jnp.sum(jnp.where(sel, q_ref[:, _tcol(t, D_MODEL)], 0.0), axis=0, keepdims=True)
            for h in range(N_XHEADS):
                q16 = jnp.where(qrow == t * N_XHEADS + h, qt[:, _tcol(h, XHEAD_DIM)], q16)
        k2 = mk_ref[xb].reshape(N_MEM * N_XHEADS, XHEAD_DIM).astype(BF16)
        v2 = mv_ref[xb].reshape(N_MEM * N_XHEADS, XHEAD_DIM).astype(BF16)
        s = jnp.where(same_head, _dot_nt(q16.astype(BF16), k2), NEG)
        m = jnp.max(s, axis=-1, keepdims=True)
        p = jnp.exp(s - m)
        l = jnp.sum(p, axis=-1, keepdims=True)
        o16 = _dot((p / l).astype(BF16), v2)
        for t in range(T_NEW):
            ot = jnp.concatenate([o16[t * N_XHEADS + h:t * N_XHEADS + h + 1, :] for h in range(N_XHEADS)], axis=1)
            cur = o_ref[:, _tcol(t, D_MODEL)]
            o_ref[:, _tcol(t, D_MODEL)] = jnp.where(sel, ot, cur)


def _s_xcore(q, mem_k, mem_v):
    bd = q.shape[0]
    assert SB % XB == 0
    rowblk = pl.BlockSpec((SB, T_NEW * D_MODEL), lambda g, i: (g, 0))
    mem = pl.BlockSpec((XB, N_MEM, N_XHEADS, XHEAD_DIM), lambda g, i: (g * (SB // XB) + i, 0, 0, 0))
    return pl.pallas_call(
        _s_xcore_kernel,
        grid=(bd // SB, SB // XB),
        in_specs=[rowblk, mem, mem],
        out_specs=rowblk,
        out_shape=jax.ShapeDtypeStruct((bd, T_NEW * D_MODEL), F32),
        compiler_params=_params(("arbitrary", "arbitrary")),
        name="s_xcore",
    )(q, mem_k, mem_v)


def _s_ffn_kernel(ox_ref, x1_ref, st_ref, wo_ref, gqx_ref, gpre_ref, wup_ref, cw_ref, wdn_ref, gpost_ref,
                  y_ref, nst_ref, g_ref):
    bd = x1_ref.shape[0]
    ox = jnp.concatenate([ox_ref[:, _tcol(t, D_MODEL)].astype(BF16) for t in range(T_NEW)], axis=0)
    yx = _dot(ox, wo_ref[...])
    x2 = [x1_ref[:, _tcol(t, D_MODEL)] + _rms(yx[t * bd:(t + 1) * bd], gqx_ref[...]) for t in range(T_NEW)]
    h = jnp.concatenate([_rms(x2[t], gpre_ref[...]).astype(BF16) for t in range(T_NEW)], axis=0)
    for c0 in range(0, D_FF, FF_CHUNK):
        cs = slice(c0, c0 + FF_CHUNK)
        act = _dot(h, wup_ref[:, cs])
        lin = _dot(h, wup_ref[:, D_FF + c0:D_FF + c0 + FF_CHUNK])
        a = [st_ref[:, r, cs] for r in range(FFN_K - 1)]
        a += [act[t * bd:(t + 1) * bd] for t in range(T_NEW)]
        for t in range(T_NEW):
            f = cw_ref[0:1, cs] * a[t] + cw_ref[1:2, cs] * a[t + 1] + cw_ref[2:3, cs] * a[t + 2]
            g_ref[t * bd:(t + 1) * bd, cs] = _ffn_act(f, lin[t * bd:(t + 1) * bd])
        for r in range(FFN_K - 1):
            nst_ref[:, r, cs] = a[T_NEW + r]
    y = _dot(g_ref[...], wdn_ref[...])
    for t in range(T_NEW):
        y_ref[:, t, :] = x2[t] + _rms(y[t * bd:(t + 1) * bd], gpost_ref[...])


def _s_ffn(ox, x1, state, wo, gqx, gpre, wup, cw, wdn, gpost):
    bd = x1.shape[0]
    outs = [jax.ShapeDtypeStruct((bd, T_NEW, D_MODEL), F32), jax.ShapeDtypeStruct(state.shape, F32)]
    return _single_step_call(_s_ffn_kernel, [ox, x1, state, wo, gqx, gpre, wup, cw, wdn, gpost], outs, "s_ffn",
                             scratch=[pltpu.VMEM((T_NEW * bd, D_FF), BF16)])


def kernel(x_prompt, x_sample, mem_prompt, cache_win_k, cache_win_v, state_conv, state_ffn_conv, cache_mem_k, cache_mem_v, g_pre_mix, w_in, conv_w, conv_b, ln_conv_g, ln_conv_b, g_out_attn, g_out_conv, w_out, g_post_mix, g_pre_x, g_mem, w_xq, w_mk, w_mv, w_xo, g_post_x, g_pre_ffn, w_up, ffn_conv_w, w_down, g_post_ffn):
    depth = w_in.shape[0]
    assert depth == 1
    B, S, _ = x_prompt.shape
    Bd, T, _ = x_sample.shape
    assert T == T_NEW and cache_win_k.shape[2] == MAX_WINDOW and S % CHUNK == 0 and Bd % SB == 0

    gpm, gpx, gm, gqx, gpf, gqf = (_row(a[0]) for a in (g_pre_mix, g_pre_x, g_mem, g_post_x, g_pre_ffn, g_post_ffn))
    gpo, goa, goc = _row(g_post_mix[0]), _row(g_out_attn[0]), _row(g_out_conv[0])
    cb, lng, lnb = _row(conv_b[0]), _row(ln_conv_g[0]), _row(ln_conv_b[0])
    cw, fcw = conv_w[0], ffn_conv_w[0]
    win, wout, wxq, wmk, wmv, wxo, wup, wdn = (
        a[0].astype(BF16) for a in (w_in, w_out, w_xq, w_mk, w_mv, w_xo, w_up, w_down))
    W = min(MAX_WINDOW, S)

    (q1, k1, v1, q4, k4, v4, q16, k16, v16, kt_p, vt_p, c_p, cst_p) = _p_inproj(x_prompt, gpm, win, cw, cb, tm=512)
    attn = _p_attn(q1, k1, v1, q4, k4, v4, q16, k16, v16)
    mk_p, mv_p, mkb, mvb = _p_memkv(mem_prompt, gm, wmk, wmv)
    x2 = _p_post(attn, c_p, x_prompt, mkb, mvb, lng, lnb, goc, goa, wout, gpo, gpx, wxq, wxo, gqx, tm=1024)
    y_p, ffn_p = _p_ffn(x2, gpf, wup, fcw, wdn, gqf, tm=512)
    conv_p = cst_p[:, HALO - (CONV_K - 1):, :]
    to_win = lambda t: jnp.transpose(t.reshape(1, B, N_HEADS, HEAD_DIM, W), (0, 1, 4, 2, 3))

    qs, ks, vs, kt_s, vt_s, us = _s_inproj(x_sample, gpm, win)
    win_kt = jnp.transpose(cache_win_k[0], (0, 2, 3, 1))
    win_vt = jnp.transpose(cache_win_v[0], (0, 2, 3, 1))
    attn_s = _s_attn(qs, ks, vs, win_kt, win_vt)
    st_conv = jnp.transpose(state_conv[0], (1, 0, 2))
    x1s, qxs, conv_s = _s_mix(attn_s, us, st_conv, x_sample, cw, cb, lng, lnb, goa, goc, wout, gpo, gpx, wxq)
    oxs = _s_xcore(qxs, cache_mem_k[0], cache_mem_v[0])
    y_s, ffn_s = _s_ffn(oxs, x1s, state_ffn_conv[0], wxo, gqx, gpf, wup, fcw, wdn, gqf)
    to_rows = lambda t: jnp.transpose(t.reshape(1, T, N_HEADS, HEAD_DIM, Bd), (0, 4, 1, 2, 3))

    return (y_p, y_s, to_win(kt_p), to_win(vt_p), conv_p[None], ffn_p[None], mk_p[None], mv_p[None],
            to_rows(kt_s), to_rows(vt_s), jnp.transpose(conv_s, (1, 0, 2))[None], ffn_s[None])
```

```python
import functools
import math

import jax
import jax.numpy as jnp
import numpy as np
from jax import lax
from jax.experimental import pallas as pl
from jax.experimental.pallas import tpu as pltpu

F32 = jnp.float32
BF16 = jnp.bfloat16

EPS = 1e-6
D_MODEL = 1024
N_HEADS = 8
HEAD_DIM = 64
ATTN_W = N_HEADS * HEAD_DIM
CONV_CH = D_MODEL - ATTN_W
CONV_K = 31
BLK = 128
DILATIONS = (16, 4, 1)
MAX_WINDOW = 2048
N_MEM = 256
N_XHEADS = 4
XHEAD_DIM = D_MODEL // N_XHEADS
D_FF = 2816
FFN_K = 3
T_NEW = 4
NEG = -1e30
LANES = 128
N_SLABS = ATTN_W // LANES
FF_CHUNK = 256
VMEM_LIMIT = 56 * 1024 * 1024

CHUNK = 16 * BLK
PHASES_PER_BRANCH = 2
TQ = CHUNK // PHASES_PER_BRANCH
CLASSES16 = 16 // PHASES_PER_BRANCH
CLASSES4 = 4 // PHASES_PER_BRANCH
ROWS4 = CHUNK // 4


def _rms(x, g):
    return x * lax.rsqrt(jnp.mean(x * x, axis=-1, keepdims=True) + EPS) * g


def _dot(a, b):
    return jnp.dot(a, b, preferred_element_type=F32)


def _dot_nt(a, b):
    return lax.dot_general(a, b, (((1,), (1,)), ((), ())), preferred_element_type=F32)


def _sigmoid(x):
    return 1.0 / (1.0 + jnp.exp(-x))


def _gelu_tanh(x):
    c = math.sqrt(2.0 / math.pi)
    return x * (0.5 * (1.0 + jnp.tanh(c * (x + 0.044715 * (x * x * x)))))


def _conv_post(c, lng, lnb):
    mu = jnp.mean(c, axis=-1, keepdims=True)
    d = c - mu
    var = jnp.mean(d * d, axis=-1, keepdims=True)
    y = d * lax.rsqrt(var + EPS) * lng + lnb
    return y * _sigmoid(y)


def _ffn_act(f, lin):
    return (_gelu_tanh(f) * lin).astype(BF16)


def _xattn_heads(qx, mk_ref, mv_ref):
    outs = []
    for h in range(N_XHEADS):
        hs = slice(h * XHEAD_DIM, (h + 1) * XHEAD_DIM)
        s = _dot_nt(qx[:, hs], mk_ref[:, hs])
        m = jnp.max(s, axis=-1, keepdims=True)
        p = jnp.exp(s - m)
        l = jnp.sum(p, axis=-1, keepdims=True)
        outs.append(_dot((p / l).astype(BF16), mv_ref[:, hs]).astype(BF16))
    return jnp.concatenate(outs, axis=-1)


def _const_spec(shape):
    nd = len(shape)
    return pl.BlockSpec(shape, lambda *_: (0,) * nd, pipeline_mode=pl.Buffered(1))


def _params(sem):
    return pltpu.CompilerParams(dimension_semantics=sem, vmem_limit_bytes=VMEM_LIMIT)


def _row(v):
    return v.reshape(1, -1).astype(F32)


def _clip(v, hi):
    return jnp.minimum(jnp.maximum(v, 0), hi)


HALO = 32
CONV_ROWS = 64


def _p_inproj_kernel(x_ref, g_ref, w_ref,
                     q1_ref, k1_ref, v1_ref, q4_ref, k4_ref, v4_ref, q16_ref, k16_ref, v16_ref,
                     kt_ref, vt_ref, u_ref, z_ref, z4_ref_all, *, tm):
    h = _rms(x_ref[...], g_ref[...]).astype(BF16)
    a = _dot(h, w_ref[:, 3 * ATTN_W:3 * ATTN_W + CONV_CH])
    gate = _dot(h, w_ref[:, 3 * ATTN_W + CONV_CH:3 * ATTN_W + 2 * CONV_CH])
    u_ref[...] = a * _sigmoid(gate)

    def project(which, scale, tok_ref, d4_ref, d16_ref, t_ref):
        z = _dot(h, w_ref[:, which * ATTN_W:(which + 1) * ATTN_W])
        if scale != 1.0:
            z = z * scale
        tok_ref[...] = z.astype(BF16)
        if t_ref is not None:
            t_ref[...] = z.T
        zs_ref, z4_ref = z_ref.at[which], z4_ref_all.at[which]
        for s in range(N_SLABS):
            ls = slice(s * LANES, (s + 1) * LANES)
            zs_ref[s] = z[:, ls]
            for r in range(4):
                z4 = zs_ref[s, pl.ds(r, tm // 4, stride=4), :]
                d4_ref[r, :, ls] = z4.astype(BF16)
                z4_ref[s, r * (tm // 4):(r + 1) * (tm // 4), :] = z4
            for r in range(16):
                z16 = z4_ref[s, pl.ds((r % 4) * (tm // 4) + r // 4, tm // 16, stride=4), :]
                d16_ref[r, :, ls] = z16.astype(BF16)

    project(0, HEAD_DIM ** -0.5 * math.log2(math.e), q1_ref, q4_ref, q16_ref, None)
    project(1, 1.0, k1_ref, k4_ref, k16_ref, kt_ref)
    project(2, 1.0, v1_ref, v4_ref, v16_ref, vt_ref)


def _p_inproj(x, g, w_in, tm):
    B, S, _ = x.shape
    W = min(MAX_WINDOW, S)
    nt = S // tm
    first = (S - W) // tm
    tok = lambda c: pl.BlockSpec((None, tm, c), lambda b, j: (b, j, 0))
    cls = lambda d: pl.BlockSpec((None, d, tm // d, ATTN_W), lambda b, j: (b, 0, j, 0))
    win = pl.BlockSpec((None, ATTN_W, tm), lambda b, j: (b, 0, jnp.maximum(j - first, 0)))
    bf = lambda *shape: jax.ShapeDtypeStruct(shape, BF16)
    consts = [g, w_in]
    return pl.pallas_call(
        functools.partial(_p_inproj_kernel, tm=tm),
        grid=(B, nt),
        in_specs=[tok(D_MODEL)] + [_const_spec(a.shape) for a in consts],
        out_specs=[tok(ATTN_W)] * 3 + [cls(4)] * 3 + [cls(16)] * 3 + [win, win, tok(CONV_CH)],
        out_shape=[bf(B, S, ATTN_W)] * 3 + [bf(B, 4, S // 4, ATTN_W)] * 3 + [bf(B, 16, S // 16, ATTN_W)] * 3
        + [jax.ShapeDtypeStruct((B, ATTN_W, W), F32)] * 2 + [jax.ShapeDtypeStruct((B, S, CONV_CH), F32)],
        scratch_shapes=[pltpu.VMEM((3, N_SLABS, tm, LANES), F32)] * 2,
        compiler_params=_params(("arbitrary", "arbitrary")),
        name="p_inproj",
    )(x, *consts)


def _p_conv_kernel(u_ref, halo_ref, cw_ref, cb_ref, c_ref, ext_ref, *, tm):
    j = pl.program_id(1)
    for s in range(N_SLABS):
        ls = slice(s * LANES, (s + 1) * LANES)
        ext_ref[s, 0:HALO, :] = jnp.where(j > 0, halo_ref[:, ls], 0.0)
        ext_ref[s, HALO:HALO + tm, :] = u_ref[:, ls]
    off = HALO - (CONV_K - 1)
    for s in range(N_SLABS):
        ls = slice(s * LANES, (s + 1) * LANES)
        for r0 in range(0, tm, CONV_ROWS):
            acc = jnp.zeros((CONV_ROWS, LANES), F32)
            for t in range(CONV_K):
                acc = acc + cw_ref[t:t + 1, ls] * ext_ref[s, r0 + off + t:r0 + off + t + CONV_ROWS, :]
            c_ref[r0:r0 + CONV_ROWS, ls] = acc + cb_ref[:, ls]


def _p_conv(u, cw, cb, tm):
    B, S, _ = u.shape
    tok = pl.BlockSpec((None, tm, CONV_CH), lambda b, j: (b, j, 0))
    halo = pl.BlockSpec((None, HALO, CONV_CH), lambda b, j: (b, jnp.maximum(j * (tm // HALO) - 1, 0), 0))
    consts = [cw, cb]
    return pl.pallas_call(
        functools.partial(_p_conv_kernel, tm=tm),
        grid=(B, S // tm),
        in_specs=[tok, halo] + [_const_spec(a.shape) for a in consts],
        out_specs=tok,
        out_shape=jax.ShapeDtypeStruct((B, S, CONV_CH), F32),
        scratch_shapes=[pltpu.VMEM((N_SLABS, HALO + tm, LANES), F32)],
        compiler_params=_params(("arbitrary", "arbitrary")),
        name="p_conv",
    )(u, u, *consts)


SM_ROWS = 32


def _pair_scores(qp, k_parts, v_parts, bias_ref, s_ref, pb_ref, mb_ref, lb_ref):
    lo = lax.broadcasted_iota(jnp.int32, (BLK, LANES), 1) < HEAD_DIM
    los = lax.broadcasted_iota(jnp.int32, (SM_ROWS, LANES), 1) < HEAD_DIM
    zq = jnp.zeros_like(qp)
    q2 = jnp.concatenate([jnp.where(lo, qp, zq), jnp.where(lo, zq, qp)], axis=0)
    s_ref[...] = _dot_nt(q2, jnp.concatenate(k_parts, axis=0))
    for r0 in range(0, BLK, SM_ROWS):
        rows = slice(r0, r0 + SM_ROWS)
        bias = bias_ref[rows, :]
        stats = []
        for hh in range(2):
            sc = s_ref[hh * BLK + r0:hh * BLK + r0 + SM_ROWS, :] + bias
            m = jnp.max(sc, axis=-1, keepdims=True)
            p = jnp.exp2(sc - m)
            stats.append((m, jnp.sum(p, axis=-1, keepdims=True)))
            pb_ref[rows, hh * 2 * BLK:(hh + 1) * 2 * BLK] = p.astype(BF16)
        mb_ref[rows, :] = jnp.where(los, stats[0][0], stats[1][0])
        lb_ref[rows, :] = jnp.where(los, stats[0][1], stats[1][1])
    pv = []
    for hh in range(2):
        pv.append(sum(_dot(pb_ref[:, (2 * hh + part) * BLK:(2 * hh + part + 1) * BLK], vp)
                      for part, vp in enumerate(v_parts)))
    return jnp.where(lo, pv[0], pv[1])


def _p_attn_kernel(q16_ref, k16_ref, k16p_ref, v16_ref, v16p_ref,
                   q4_ref, k4_ref, k4p_ref, v4_ref, v4p_ref,
                   q1_ref, k1_ref, k1p_ref, v1_ref, v1p_ref,
                   o_ref, m_ref, l_ref, a_ref, bias_ref, s_ref, pb_ref, mb_ref, lb_ref):
    c = pl.program_id(1)
    ph = pl.program_id(2)

    @pl.when((c == 0) & (ph == 0))
    def _():
        row = lax.broadcasted_iota(jnp.int32, (BLK, 2 * BLK), 0)
        col = lax.broadcasted_iota(jnp.int32, (BLK, 2 * BLK), 1)
        band = (col >= row) & (col <= row + BLK)
        bias_ref[0] = jnp.where(band, 0.0, NEG)
        bias_ref[1] = jnp.where(band & (col >= BLK), 0.0, NEG)

    def unit(qp, kk, vv, seq_start, state_rows, first, final, out_rows=None):
        pair, rows = state_rows
        bias = bias_ref.at[jnp.where(seq_start, 1, 0)] if not isinstance(seq_start, bool) else bias_ref.at[int(seq_start)]
        buf = pair % 2
        pv = _pair_scores(qp, kk, vv, bias, s_ref.at[buf], pb_ref.at[buf], mb_ref.at[buf], lb_ref.at[buf])
        m_b = mb_ref[buf]
        l_b = lb_ref[buf]
        if first:
            m_new, l_new, a_new = m_b, l_b, pv
        else:
            m0 = m_ref[pair, rows, :]
            m_new = jnp.maximum(m0, m_b)
            a0 = jnp.exp2(m0 - m_new)
            a1 = jnp.exp2(m_b - m_new)
            l_new = a0 * l_ref[pair, rows, :] + a1 * l_b
            a_new = a0 * a_ref[pair, rows, :] + a1 * pv
        if final:
            o_ref[out_rows, pair * LANES:(pair + 1) * LANES] = (a_new / l_new).astype(o_ref.dtype)
        else:
            m_ref[pair, rows, :] = m_new
            l_ref[pair, rows, :] = l_new
            a_ref[pair, rows, :] = a_new

    @pl.when(ph < PHASES_PER_BRANCH)
    def _():
        for cl in range(CLASSES16):
            r = ph * CLASSES16 + cl
            for pair in range(N_SLABS):
                ls = slice(pair * LANES, (pair + 1) * LANES)
                kk = (k16p_ref[cl, :, ls], k16_ref[cl, :, ls])
                vv = (v16p_ref[cl, :, ls], v16_ref[cl, :, ls])
                unit(q16_ref[cl, :, ls], kk, vv, c == 0, (pair, pl.ds(r, BLK, stride=16)), first=True, final=False)

    def contiguous_branch(nblk, q_ref, k_ref, kp_ref, v_ref, vp_ref, seq_start, rows_of, final):
        for n in range(nblk):
            for pair in range(N_SLABS):
                ls = slice(pair * LANES, (pair + 1) * LANES)
                if n == 0:
                    kk = (kp_ref[:, ls], k_ref[0:BLK, ls])
                    vv = (vp_ref[:, ls], v_ref[0:BLK, ls])
                else:
                    kk = (k_ref[(n - 1) * BLK:n * BLK, ls], k_ref[n * BLK:(n + 1) * BLK, ls])
                    vv = (v_ref[(n - 1) * BLK:n * BLK, ls], v_ref[n * BLK:(n + 1) * BLK, ls])
                unit(q_ref[n * BLK:(n + 1) * BLK, ls], kk, vv, seq_start if n == 0 else False, (pair, rows_of(n)),
                     first=False, final=final, out_rows=slice(n * BLK, (n + 1) * BLK))

    @pl.when((ph >= PHASES_PER_BRANCH) & (ph < 2 * PHASES_PER_BRANCH))
    def _():
        for cl in range(CLASSES4):
            r = (ph - PHASES_PER_BRANCH) * CLASSES4 + cl
            contiguous_branch(ROWS4 // BLK, q4_ref.at[cl], k4_ref.at[cl], k4p_ref.at[cl], v4_ref.at[cl],
                              v4p_ref.at[cl], c == 0, lambda n: pl.ds(4 * BLK * n + r, BLK, stride=4), final=False)

    @pl.when(ph >= 2 * PHASES_PER_BRANCH)
    def _():
        i = ph - 2 * PHASES_PER_BRANCH
        contiguous_branch(TQ // BLK, q1_ref, k1_ref, k1p_ref, v1_ref, v1p_ref, (c == 0) & (i == 0),
                          lambda n: pl.ds(pl.multiple_of(i * TQ + n * BLK, BLK), BLK), final=True)


def _p_attn(q1, k1, v1, q4, k4, v4, q16, k16, v16):
    B, S, _ = q1.shape
    P = PHASES_PER_BRANCH
    assert S % CHUNK == 0 and TQ % BLK == 0
    d16 = pl.BlockSpec((None, CLASSES16, BLK, ATTN_W), lambda b, c, p: (b, _clip(p, P - 1), c, 0))
    d16p = pl.BlockSpec((None, CLASSES16, BLK, ATTN_W),
                        lambda b, c, p: (b, _clip(p, P - 1), jnp.maximum(c - 1, 0), 0))
    d4 = pl.BlockSpec((None, CLASSES4, ROWS4, ATTN_W), lambda b, c, p: (b, _clip(p - P, P - 1), c, 0))
    d4p = pl.BlockSpec((None, CLASSES4, BLK, ATTN_W),
                       lambda b, c, p: (b, _clip(p - P, P - 1), jnp.maximum(c * (ROWS4 // BLK) - 1, 0), 0))
    d1 = pl.BlockSpec((None, TQ, ATTN_W), lambda b, c, p: (b, c * P + _clip(p - 2 * P, P - 1), 0))
    d1p = pl.BlockSpec((None, BLK, ATTN_W),
                       lambda b, c, p: (b, jnp.maximum((c * P + _clip(p - 2 * P, P - 1)) * (TQ // BLK) - 1, 0), 0))
    return pl.pallas_call(
        _p_attn_kernel,
        grid=(B, S // CHUNK, 3 * P),
        in_specs=[d16, d16, d16p, d16, d16p, d4, d4, d4p, d4, d4p, d1, d1, d1p, d1, d1p],
        out_specs=d1,
        out_shape=jax.ShapeDtypeStruct((B, S, ATTN_W), BF16),
        scratch_shapes=[pltpu.VMEM((N_SLABS, CHUNK, LANES), F32)] * 3
        + [pltpu.VMEM((2, BLK, 2 * BLK), F32), pltpu.VMEM((2, 2 * BLK, 2 * BLK), F32),
           pltpu.VMEM((2, BLK, 4 * BLK), BF16), pltpu.VMEM((2, BLK, LANES), F32), pltpu.VMEM((2, BLK, LANES), F32)],
        compiler_params=_params(("arbitrary",) * 3),
        name="p_attn",
    )(q16, k16, k16, v16, v16, q4, k4, k4, v4, v4, q1, k1, k1, v1, v1)


def _p_memkv_kernel(mem_ref, g_ref, wk_ref, wv_ref, k_ref, v_ref, kb_ref, vb_ref):
    h = _rms(mem_ref[...], g_ref[...]).astype(BF16)
    k = _dot(h, wk_ref[...])
    v = _dot(h, wv_ref[...])
    kb_ref[...] = k.astype(BF16)
    vb_ref[...] = v.astype(BF16)
    for hd in range(N_XHEADS):
        k_ref[:, hd, :] = k[:, hd * XHEAD_DIM:(hd + 1) * XHEAD_DIM]
        v_ref[:, hd, :] = v[:, hd * XHEAD_DIM:(hd + 1) * XHEAD_DIM]


def _p_memkv(mem, g, wk, wv):
    B, M, _ = mem.shape
    blk = pl.BlockSpec((None, M, D_MODEL), lambda b: (b, 0, 0))
    oblk = pl.BlockSpec((None, M, N_XHEADS, XHEAD_DIM), lambda b: (b, 0, 0, 0))
    return pl.pallas_call(
        _p_memkv_kernel,
        grid=(B,),
        in_specs=[blk, _const_spec(g.shape), _const_spec(wk.shape), _const_spec(wv.shape)],
        out_specs=[oblk, oblk, blk, blk],
        out_shape=[jax.ShapeDtypeStruct((B, M, N_XHEADS, XHEAD_DIM), F32)] * 2
        + [jax.ShapeDtypeStruct((B, M, D_MODEL), BF16)] * 2,
        compiler_params=_params(("arbitrary",)),
        name="p_memkv",
    )(mem, g, wk, wv)


def _p_post_kernel(attn_ref, c_ref, x_ref, mk_ref, mv_ref, lng_ref, lnb_ref, goc_ref, goa_ref, wout_ref, gpm_ref,
                   gpre_ref, wq_ref, wo_ref, gpost_ref, o_ref):
    ma = _rms(attn_ref[...].astype(F32), goa_ref[...]).astype(BF16)
    mc = _rms(_conv_post(c_ref[...], lng_ref[...], lnb_ref[...]), goc_ref[...]).astype(BF16)
    y = _dot(ma, wout_ref[0:ATTN_W, :]) + _dot(mc, wout_ref[ATTN_W:D_MODEL, :])
    x1 = x_ref[...] + _rms(y, gpm_ref[...])
    h = _rms(x1, gpre_ref[...]).astype(BF16)
    qx = (_dot(h, wq_ref[...]) * (XHEAD_DIM ** -0.5)).astype(BF16)
    ox = _xattn_heads(qx, mk_ref, mv_ref)
    o_ref[...] = x1 + _rms(_dot(ox, wo_ref[...]), gpost_ref[...])


def _p_post(attn, c, x, mk, mv, lng, lnb, goc, goa, wout, gpm, gpre, wq, wo, gpost, tm):
    B, S, _ = x.shape
    tok = lambda c: pl.BlockSpec((None, tm, c), lambda b, j: (b, j, 0))
    mem = pl.BlockSpec((None, N_MEM, D_MODEL), lambda b, j: (b, 0, 0))
    consts = [lng, lnb, goc, goa, wout, gpm, gpre, wq, wo, gpost]
    return pl.pallas_call(
        _p_post_kernel,
        grid=(B, S // tm),
        in_specs=[tok(ATTN_W), tok(CONV_CH), tok(D_MODEL), mem, mem] + [_const_spec(a.shape) for a in consts],
        out_specs=tok(D_MODEL),
        out_shape=jax.ShapeDtypeStruct((B, S, D_MODEL), F32),
        compiler_params=_params(("arbitrary", "arbitrary")),
        name="p_post",
    )(attn, c, x, mk, mv, *consts)


FFN_PAD = 8


def _p_ffn_kernel(x_ref, gpre_ref, wup_ref, cw_ref, wdn_ref, gpost_ref, y_ref, st_ref, ext_ref, g_ref, *, tm):
    j = pl.program_id(1)

    @pl.when(j == 0)
    def _():
        ext_ref[0:FFN_PAD, :] = jnp.zeros((FFN_PAD, D_FF), F32)

    x = x_ref[...]
    h = _rms(x, gpre_ref[...]).astype(BF16)
    for c0 in range(0, D_FF, FF_CHUNK):
        cs = slice(c0, c0 + FF_CHUNK)
        ext_ref[FFN_PAD:FFN_PAD + tm, cs] = _dot(h, wup_ref[:, cs])
        lin = _dot(h, wup_ref[:, D_FF + c0:D_FF + c0 + FF_CHUNK])
        f = (cw_ref[0:1, cs] * ext_ref[FFN_PAD - 2:FFN_PAD - 2 + tm, cs]
             + cw_ref[1:2, cs] * ext_ref[FFN_PAD - 1:FFN_PAD - 1 + tm, cs]
             + cw_ref[2:3, cs] * ext_ref[FFN_PAD:FFN_PAD + tm, cs])
        g_ref[:, cs] = _ffn_act(f, lin)
    tail = ext_ref[tm:tm + FFN_PAD, :]
    ext_ref[0:FFN_PAD, :] = tail
    st_ref[...] = tail[FFN_PAD - (FFN_K - 1):FFN_PAD, :]
    y_ref[...] = x + _rms(_dot(g_ref[...], wdn_ref[...]), gpost_ref[...])


def _p_ffn(x, gpre, wup, cw, wdn, gpost, tm):
    B, S, _ = x.shape
    tok = pl.BlockSpec((None, tm, D_MODEL), lambda b, j: (b, j, 0))
    st = pl.BlockSpec((None, FFN_K - 1, D_FF), lambda b, j: (b, 0, 0))
    return pl.pallas_call(
        functools.partial(_p_ffn_kernel, tm=tm),
        grid=(B, S // tm),
        in_specs=[tok, _const_spec(gpre.shape), _const_spec(wup.shape), _const_spec(cw.shape),
                  _const_spec(wdn.shape), _const_spec(gpost.shape)],
        out_specs=[tok, st],
        out_shape=[jax.ShapeDtypeStruct((B, S, D_MODEL), F32), jax.ShapeDtypeStruct((B, FFN_K - 1, D_FF), F32)],
        scratch_shapes=[pltpu.VMEM((FFN_PAD + tm, D_FF), F32), pltpu.VMEM((tm, D_FF), BF16)],
        compiler_params=_params(("arbitrary", "arbitrary")),
        name="p_ffn",
    )(x, gpre, wup, cw, wdn, gpost)


def _tcol(t, c):
    return slice(t * c, (t + 1) * c)


def _single_step_call(kernel, ins, out_shapes, name, scratch=()):
    return pl.pallas_call(
        kernel,
        grid=(1,),
        in_specs=[_const_spec(a.shape) for a in ins],
        out_specs=[pl.BlockSpec(s.shape, lambda *_, nd=len(s.shape): (0,) * nd) for s in out_shapes],
        out_shape=out_shapes,
        scratch_shapes=list(scratch),
        compiler_params=_params(("arbitrary",)),
        name=name,
    )(*ins)


def _s_inproj_kernel(x_ref, g_ref, w_ref, q_ref, k_ref, v_ref, kt_ref, vt_ref, u_ref):
    bd = x_ref.shape[0]
    g = g_ref[...]
    h = jnp.concatenate([_rms(x_ref[:, t, :], g).astype(BF16) for t in range(T_NEW)], axis=0)
    q = _dot(h, w_ref[:, 0:ATTN_W]) * (HEAD_DIM ** -0.5)
    k = _dot(h, w_ref[:, ATTN_W:2 * ATTN_W])
    v = _dot(h, w_ref[:, 2 * ATTN_W:3 * ATTN_W])
    a = _dot(h, w_ref[:, 3 * ATTN_W:3 * ATTN_W + CONV_CH])
    gate = _dot(h, w_ref[:, 3 * ATTN_W + CONV_CH:3 * ATTN_W + 2 * CONV_CH])
    u = a * _sigmoid(gate)
    for t in range(T_NEW):
        rows = slice(t * bd, (t + 1) * bd)
        q_ref[:, _tcol(t, ATTN_W)] = q[rows]
        k_ref[:, _tcol(t, ATTN_W)] = k[rows]
        v_ref[:, _tcol(t, ATTN_W)] = v[rows]
        kt_ref[t] = k[rows].T
        vt_ref[t] = v[rows].T
        u_ref[:, _tcol(t, CONV_CH)] = u[rows]


def _s_inproj(x, g, w_in):
    bd = x.shape[0]
    flat = jax.ShapeDtypeStruct((bd, T_NEW * ATTN_W), F32)
    tr = jax.ShapeDtypeStruct((T_NEW, ATTN_W, bd), F32)
    return _single_step_call(_s_inproj_kernel, [x, g, w_in], [flat, flat, flat, tr, tr, flat], "s_inproj")


SB = 8
NEW_PAD = 8


def _sample_key_multiplicity():
    t = np.arange(T_NEW)[:, None]
    dist = MAX_WINDOW + t - np.arange(MAX_WINDOW)[None, :]
    mult = np.zeros((T_NEW, MAX_WINDOW), np.float32)
    for d in DILATIONS:
        mult += (dist % d == 0) & (dist // d >= 1) & (dist // d <= BLK)
    tn = np.arange(NEW_PAD)[None, :]
    new = np.where(tn == t, float(len(DILATIONS)), np.where(tn < t, 1.0, 0.0)).astype(np.float32)
    return np.repeat(mult, N_HEADS, axis=0), np.repeat(new, N_HEADS, axis=0)


def _s_attn_kernel(q_ref, kn_ref, vn_ref, kt_ref, vt_ref, mw_ref, mn_ref, o_ref):
    ii = pl.program_id(1)
    rows8 = lax.broadcasted_iota(jnp.int32, (SB, ATTN_W), 0)
    sel = rows8 == ii

    @pl.when(ii == 0)
    def _():
        o_ref[...] = jnp.zeros(o_ref.shape, F32)

    def pick(ref, t):
        return jnp.sum(jnp.where(sel, ref[:, _tcol(t, ATTN_W)], 0.0), axis=0, keepdims=True)

    head_of_lane = lax.broadcasted_iota(jnp.int32, (N_HEADS, ATTN_W), 1) // HEAD_DIM
    diag = head_of_lane == lax.broadcasted_iota(jnp.int32, (N_HEADS, ATTN_W), 0)
    qbd = jnp.concatenate([jnp.where(diag, pick(q_ref, t), 0.0) for t in range(T_NEW)], axis=0).astype(BF16)
    pad = jnp.zeros((NEW_PAD - T_NEW, ATTN_W), F32)
    k_new = jnp.concatenate([pick(kn_ref, t) for t in range(T_NEW)] + [pad], axis=0).astype(BF16)
    v_new = jnp.concatenate([pick(vn_ref, t) for t in range(T_NEW)] + [pad], axis=0).astype(BF16)

    kt = kt_ref[...].reshape(ATTN_W, MAX_WINDOW).astype(BF16)
    vt = vt_ref[...].reshape(ATTN_W, MAX_WINDOW).astype(BF16)
    mw = mw_ref[...]
    mn = mn_ref[...]
    s_w = jnp.where(mw > 0.0, _dot(qbd, kt), NEG)
    s_n = jnp.where(mn > 0.0, _dot_nt(qbd, k_new), NEG)
    m = jnp.maximum(jnp.max(s_w, axis=-1, keepdims=True), jnp.max(s_n, axis=-1, keepdims=True))
    p_w = jnp.exp(s_w - m) * mw
    p_n = jnp.exp(s_n - m) * mn
    l = jnp.sum(p_w, axis=-1, keepdims=True) + jnp.sum(p_n, axis=-1, keepdims=True)
    o_full = _dot_nt((p_w / l).astype(BF16), vt) + _dot((p_n / l).astype(BF16), v_new)
    for t in range(T_NEW):
        o = jnp.sum(jnp.where(diag, o_full[t * N_HEADS:(t + 1) * N_HEADS], 0.0), axis=0, keepdims=True)
        cur = o_ref[:, _tcol(t, ATTN_W)]
        o_ref[:, _tcol(t, ATTN_W)] = jnp.where(sel, o, cur)


def _s_attn(q, kn, vn, win_kt, win_vt):
    bd = q.shape[0]
    mw, mn = _sample_key_multiplicity()
    rowblk = pl.BlockSpec((SB, T_NEW * ATTN_W), lambda g, i: (g, 0))
    win = pl.BlockSpec((None, N_HEADS, HEAD_DIM, MAX_WINDOW), lambda g, i: (g * SB + i, 0, 0, 0))
    return pl.pallas_call(
        _s_attn_kernel,
        grid=(bd // SB, SB),
        in_specs=[rowblk, rowblk, rowblk, win, win, _const_spec(mw.shape), _const_spec(mn.shape)],
        out_specs=rowblk,
        out_shape=jax.ShapeDtypeStruct((bd, T_NEW * ATTN_W), F32),
        compiler_params=_params(("arbitrary", "arbitrary")),
        name="s_attn",
    )(q, kn, vn, win_kt, win_vt, jnp.asarray(mw), jnp.asarray(mn))


def _s_mix_kernel(attn_ref, u_ref, st_ref, x_ref, cw_ref, cb_ref, lng_ref, lnb_ref, goa_ref, goc_ref,
                  wout_ref, gpm_ref, gpre_ref, wq_ref, x1_ref, qx_ref, nst_ref):
    bd = x_ref.shape[0]
    nctx = CONV_K - 1

    def ctx(r):
        return st_ref[r] if r < nctx else u_ref[:, _tcol(r - nctx, CONV_CH)]

    mcs, mas = [], []
    for t in range(T_NEW):
        acc = jnp.zeros((bd, CONV_CH), F32)
        for j in range(CONV_K):
            acc = acc + cw_ref[j:j + 1, :] * ctx(t + j)
        c = _conv_post(acc + cb_ref[...], lng_ref[...], lnb_ref[...])
        mcs.append(_rms(c, goc_ref[...]).astype(BF16))
        mas.append(_rms(attn_ref[:, _tcol(t, ATTN_W)], goa_ref[...]).astype(BF16))
    y = (_dot(jnp.concatenate(mas, axis=0), wout_ref[0:ATTN_W, :])
         + _dot(jnp.concatenate(mcs, axis=0), wout_ref[ATTN_W:D_MODEL, :]))
    hs = []
    for t in range(T_NEW):
        x1 = x_ref[:, t, :] + _rms(y[t * bd:(t + 1) * bd], gpm_ref[...])
        x1_ref[:, _tcol(t, D_MODEL)] = x1
        hs.append(_rms(x1, gpre_ref[...]).astype(BF16))
    q = _dot(jnp.concatenate(hs, axis=0), wq_ref[...]) * (XHEAD_DIM ** -0.5)
    for t in range(T_NEW):
        qx_ref[:, _tcol(t, D_MODEL)] = q[t * bd:(t + 1) * bd]
    for r in range(nctx):
        nst_ref[r] = ctx(r + T_NEW)


def _s_mix(attn, u, state, x, cw, cb, lng, lnb, goa, goc, wout, gpm, gpre, wq):
    bd = x.shape[0]
    full = jax.ShapeDtypeStruct((bd, T_NEW * D_MODEL), F32)
    outs = [full, full, jax.ShapeDtypeStruct(state.shape, F32)]
    ins = [attn, u, state, x, cw, cb, lng, lnb, goa, goc, wout, gpm, gpre, wq]
    return _single_step_call(_s_mix_kernel, ins, outs, "s_mix")


XB = 4


def _s_xcore_kernel(q_ref, mk_ref, mv_ref, o_ref):
    ii = pl.program_id(1)
    rows8 = lax.broadcasted_iota(jnp.int32, (SB, D_MODEL), 0)

    @pl.when(ii == 0)
    def _():
        o_ref[...] = jnp.zeros(o_ref.shape, F32)

    nq = T_NEW * N_XHEADS
    qrow = lax.broadcasted_iota(jnp.int32, (nq, XHEAD_DIM), 0)
    srow = lax.broadcasted_iota(jnp.int32, (nq, N_MEM * N_XHEADS), 0)
    scol = lax.broadcasted_iota(jnp.int32, (nq, N_MEM * N_XHEADS), 1)
    same_head = srow % N_XHEADS == scol % N_XHEADS
    for xb in range(XB):
        sel = rows8 == ii * XB + xb
        q16 = jnp.zeros((nq, XHEAD_DIM), F32)
        for t in range(T_NEW):
            qt = jnp.sum(jnp.where(sel, q_ref[:, _tcol(t, D_MODEL)], 0.0), axis=0, keepdims=True)
            for h in range(N_XHEADS):
                q16 = jnp.where(qrow == t * N_XHEADS + h, qt[:, _tcol(h, XHEAD_DIM)], q16)
        k2 = mk_ref[xb].reshape(N_MEM * N_XHEADS, XHEAD_DIM).astype(BF16)
        v2 = mv_ref[xb].reshape(N_MEM * N_XHEADS, XHEAD_DIM).astype(BF16)
        s = jnp.where(same_head, _dot_nt(q16.astype(BF16), k2), NEG)
        m = jnp.max(s, axis=-1, keepdims=True)
        p = jnp.exp(s - m)
        l = jnp.sum(p, axis=-1, keepdims=True)
        o16 = _dot((p / l).astype(BF16), v2)
        for t in range(T_NEW):
            ot = jnp.concatenate([o16[t * N_XHEADS + h:t * N_XHEADS + h + 1, :] for h in range(N_XHEADS)], axis=1)
            cur = o_ref[:, _tcol(t, D_MODEL)]
            o_ref[:, _tcol(t, D_MODEL)] = jnp.where(sel, ot, cur)


def _s_xcore(q, mem_k, mem_v):
    bd = q.shape[0]
    assert SB % XB == 0
    rowblk = pl.BlockSpec((SB, T_NEW * D_MODEL), lambda g, i: (g, 0))
    mem = pl.BlockSpec((XB, N_MEM, N_XHEADS, XHEAD_DIM), lambda g, i: (g * (SB // XB) + i, 0, 0, 0))
    return pl.pallas_call(
        _s_xcore_kernel,
        grid=(bd // SB, SB // XB),
        in_specs=[rowblk, mem, mem],
        out_specs=rowblk,
        out_shape=jax.ShapeDtypeStruct((bd, T_NEW * D_MODEL), F32),
        compiler_params=_params(("arbitrary", "arbitrary")),
        name="s_xcore",
    )(q, mem_k, mem_v)


def _s_ffn_kernel(ox_ref, x1_ref, st_ref, wo_ref, gqx_ref, gpre_ref, wup_ref, cw_ref, wdn_ref, gpost_ref,
                  y_ref, nst_ref, g_ref):
    bd = x1_ref.shape[0]
    ox = jnp.concatenate([ox_ref[:, _tcol(t, D_MODEL)].astype(BF16) for t in range(T_NEW)], axis=0)
    yx = _dot(ox, wo_ref[...])
    x2 = [x1_ref[:, _tcol(t, D_MODEL)] + _rms(yx[t * bd:(t + 1) * bd], gqx_ref[...]) for t in range(T_NEW)]
    h = jnp.concatenate([_rms(x2[t], gpre_ref[...]).astype(BF16) for t in range(T_NEW)], axis=0)
    for c0 in range(0, D_FF, FF_CHUNK):
        cs = slice(c0, c0 + FF_CHUNK)
        act = _dot(h, wup_ref[:, cs])
        lin = _dot(h, wup_ref[:, D_FF + c0:D_FF + c0 + FF_CHUNK])
        a = [st_ref[:, r, cs] for r in range(FFN_K - 1)]
        a += [act[t * bd:(t + 1) * bd] for t in range(T_NEW)]
        for t in range(T_NEW):
            f = cw_ref[0:1, cs] * a[t] + cw_ref[1:2, cs] * a[t + 1] + cw_ref[2:3, cs] * a[t + 2]
            g_ref[t * bd:(t + 1) * bd, cs] = _ffn_act(f, lin[t * bd:(t + 1) * bd])
        for r in range(FFN_K - 1):
            nst_ref[:, r, cs] = a[T_NEW + r]
    y = _dot(g_ref[...], wdn_ref[...])
    for t in range(T_NEW):
        y_ref[:, t, :] = x2[t] + _rms(y[t * bd:(t + 1) * bd], gpost_ref[...])


def _s_ffn(ox, x1, state, wo, gqx, gpre, wup, cw, wdn, gpost):
    bd = x1.shape[0]
    outs = [jax.ShapeDtypeStruct((bd, T_NEW, D_MODEL), F32), jax.ShapeDtypeStruct(state.shape, F32)]
    return _single_step_call(_s_ffn_kernel, [ox, x1, state, wo, gqx, gpre, wup, cw, wdn, gpost], outs, "s_ffn",
                             scratch=[pltpu.VMEM((T_NEW * bd, D_FF), BF16)])


def kernel(x_prompt, x_sample, mem_prompt, cache_win_k, cache_win_v, state_conv, state_ffn_conv, cache_mem_k, cache_mem_v, g_pre_mix, w_in, conv_w, conv_b, ln_conv_g, ln_conv_b, g_out_attn, g_out_conv, w_out, g_post_mix, g_pre_x, g_mem, w_xq, w_mk, w_mv, w_xo, g_post_x, g_pre_ffn, w_up, ffn_conv_w, w_down, g_post_ffn):
    depth = w_in.shape[0]
    assert depth == 1
    B, S, _ = x_prompt.shape
    Bd, T, _ = x_sample.shape
    assert T == T_NEW and cache_win_k.shape[2] == MAX_WINDOW and S % CHUNK == 0 and Bd % SB == 0

    gpm, gpx, gm, gqx, gpf, gqf = (_row(a[0]) for a in (g_pre_mix, g_pre_x, g_mem, g_post_x, g_pre_ffn, g_post_ffn))
    gpo, goa, goc = _row(g_post_mix[0]), _row(g_out_attn[0]), _row(g_out_conv[0])
    cb, lng, lnb = _row(conv_b[0]), _row(ln_conv_g[0]), _row(ln_conv_b[0])
    cw, fcw = conv_w[0], ffn_conv_w[0]
    win, wout, wxq, wmk, wmv, wxo, wup, wdn = (
        a[0].astype(BF16) for a in (w_in, w_out, w_xq, w_mk, w_mv, w_xo, w_up, w_down))
    W = min(MAX_WINDOW, S)

    (q1, k1, v1, q4, k4, v4, q16, k16, v16, kt_p, vt_p, u_p) = _p_inproj(x_prompt, gpm, win, tm=512)
    c_p = _p_conv(u_p, cw, cb, tm=1024)
    attn = _p_attn(q1, k1, v1, q4, k4, v4, q16, k16, v16)
    mk_p, mv_p, mkb, mvb = _p_memkv(mem_prompt, gm, wmk, wmv)
    x2 = _p_post(attn, c_p, x_prompt, mkb, mvb, lng, lnb, goc, goa, wout, gpo, gpx, wxq, wxo, gqx, tm=1024)
    y_p, ffn_p = _p_ffn(x2, gpf, wup, fcw, wdn, gqf, tm=512)
    conv_p = u_p[:, S - (CONV_K - 1):, :]
    to_win = lambda t: jnp.transpose(t.reshape(1, B, N_HEADS, HEAD_DIM, W), (0, 1, 4, 2, 3))

    qs, ks, vs, kt_s, vt_s, us = _s_inproj(x_sample, gpm, win)
    win_kt = jnp.transpose(cache_win_k[0], (0, 2, 3, 1))
    win_vt = jnp.transpose(cache_win_v[0], (0, 2, 3, 1))
    attn_s = _s_attn(qs, ks, vs, win_kt, win_vt)
    st_conv = jnp.transpose(state_conv[0], (1, 0, 2))
    x1s, qxs, conv_s = _s_mix(attn_s, us, st_conv, x_sample, cw, cb, lng, lnb, goa, goc, wout, gpo, gpx, wxq)
    oxs = _s_xcore(qxs, cache_mem_k[0], cache_mem_v[0])
    y_s, ffn_s = _s_ffn(oxs, x1s, state_ffn_conv[0], wxo, gqx, gpf, wup, fcw, wdn, gqf)
    to_rows = lambda t: jnp.transpose(t.reshape(1, T, N_HEADS, HEAD_DIM, Bd), (0, 4, 1, 2, 3))

    return (y_p, y_s, to_win(kt_p), to_win(vt_p), conv_p[None], ffn_p[None], mk_p[None], mv_p[None],
            to_rows(kt_s), to_rows(vt_s), jnp.transpose(conv_s, (1, 0, 2))[None], ffn_s[None])
```
